```python
import math
import jax, jax.numpy as jnp
from jax import lax
import numpy as np

D_MODEL = 1024
BATCH = 2
SEQ = 16384
DEPTH = 2

CHUNK = 64
SSM_GROUP = 16
SSM_GROUPS = 16
D_SSM = SSM_GROUP * SSM_GROUPS
SSM_STATE = 64
SB_HEADS = 8
SB_HEAD_DIM = 64
D_SB = SB_HEADS * SB_HEAD_DIM
D_CONV = 256
CONV_WIDTH = 3
D_FF = 4 * D_MODEL
QBLOCK = 128
EPS = 1e-6
DT_MIN = 1e-3
DT_MAX = 1e-1
N_BRANCH = 3
D_IN = D_SSM + 3 * D_SB + 3 * D_CONV
SPLITS = [D_SSM, D_SSM + D_SB, D_SSM + 2 * D_SB, D_SSM + 3 * D_SB,
          D_SSM + 3 * D_SB + D_CONV, D_SSM + 3 * D_SB + 2 * D_CONV]

kernel_name = "hybrid_s5_stickbreak_shortconv_block"


def rmsnorm(x, g):
    x32 = x.astype(jnp.float32)
    y = x32 * lax.rsqrt(jnp.mean(x32 * x32, axis=-1, keepdims=True) + EPS) * g.astype(jnp.float32)
    return y.astype(x.dtype)


def _complex_linear_combine(e1, e2):
    a1r, a1i, b1r, b1i = e1
    a2r, a2i, b2r, b2i = e2
    ar = a1r * a2r - a1i * a2i
    ai = a1r * a2i + a1i * a2r
    br = a2r * b1r - a2i * b1i + b2r
    bi = a2r * b1i + a2i * b1r + b2i
    return (ar, ai, br, bi)


def s5_branch(u, a_re, a_im, log_dt, b_re, b_im, c_re, c_im, d_skip, w_val, w_gate):
    bsz, s, _ = u.shape
    f32 = jnp.float32
    uf = u.astype(f32)
    ug = uf.reshape(bsz, s, SSM_GROUPS, SSM_GROUP)
    dt = jnp.exp(log_dt.astype(f32))[:, None]
    lr = a_re.astype(f32)
    li = a_im.astype(f32)
    mag = jnp.exp(lr * dt)
    ab_re = mag * jnp.cos(li * dt)
    ab_im = mag * jnp.sin(li * dt)
    den = lr * lr + li * li
    xr = ab_re - 1.0
    coef_re = (xr * lr + ab_im * li) / den
    coef_im = (ab_im * lr - xr * li) / den
    br = b_re.astype(f32)
    bi = b_im.astype(f32)
    bb_re = coef_re[..., None] * br - coef_im[..., None] * bi
    bb_im = coef_re[..., None] * bi + coef_im[..., None] * br
    bu_re = jnp.einsum('bsgc,gnc->bsgn', ug, bb_re)
    bu_im = jnp.einsum('bsgc,gnc->bsgn', ug, bb_im)
    a_r = jnp.broadcast_to(ab_re, bu_re.shape)
    a_i = jnp.broadcast_to(ab_im, bu_re.shape)
    _, _, h_re, h_im = lax.associative_scan(_complex_linear_combine, (a_r, a_i, bu_re, bu_im), axis=1)
    y = (jnp.einsum('bsgn,gcn->bsgc', h_re, c_re.astype(f32))
         - jnp.einsum('bsgn,gcn->bsgc', h_im, c_im.astype(f32)))
    y = y.reshape(bsz, s, D_SSM) + d_skip.astype(f32) * uf
    y = jax.nn.gelu(y).astype(u.dtype)
    return (y @ w_val) * jax.nn.sigmoid(y @ w_gate)


def stick_breaking_attention(q, k, v):
    s_len = q.shape[2]
    scale = SB_HEAD_DIM ** -0.5
    outs = []
    for i in range(s_len // QBLOCK):
        t0 = i * QBLOCK
        t1 = t0 + QBLOCK
        kb = k[:, :, :t1]
        vb = v[:, :, :t1]
        z = jnp.einsum('bhtd,bhsd->bhts', q[:, :, t0:t1], kb).astype(jnp.float32) * scale
        tpos = t0 + jnp.arange(QBLOCK)[:, None]
        spos = jnp.arange(t1)[None, :]
        mask = spos < tpos
        log_1m = jnp.where(mask, jax.nn.log_sigmoid(-z), 0.0)
        suffix = lax.cumsum(log_1m, axis=log_1m.ndim - 1, reverse=True) - log_1m
        w = jnp.where(mask, jnp.exp(jax.nn.log_sigmoid(z) + suffix), 0.0)
        outs.append(jnp.einsum('bhts,bhsd->bhtd', w.astype(v.dtype), vb))
    return jnp.concatenate(outs, axis=2)


def short_conv_branch(gate_b, gate_c, x_in, conv_w, w_out):
    z = gate_c * x_in
    conv_filter = conv_w[:, None, :].astype(z.dtype)
    y = lax.conv_general_dilated(z, conv_filter, window_strides=(1,),
                                 padding=[(CONV_WIDTH - 1, 0)],
                                 dimension_numbers=('NWC', 'WIO', 'NWC'),
                                 feature_group_count=D_CONV)
    return (gate_b * y) @ w_out


def setup_inputs(seed: int = 0) -> dict:
    key = jax.random.key(seed)
    ks = jax.random.split(key, 24)
    f32 = jnp.float32

    def nrm(k, shape, scale):
        return jax.random.normal(k, shape, f32) * scale

    n_idx = jnp.arange(SSM_STATE, dtype=f32)
    return {
        "x": nrm(ks[0], (BATCH, SEQ, D_MODEL), 1.0),
        "norm_mix_pre": 1.0 + nrm(ks[1], (DEPTH, D_MODEL), 0.02),
        "norm_mix_post": 1.0 + nrm(ks[2], (DEPTH, D_MODEL), 0.02),
        "w_in": nrm(ks[3], (DEPTH, D_MODEL, D_IN), D_MODEL ** -0.5),
        "w_gate": nrm(ks[4], (DEPTH, D_MODEL, N_BRANCH * D_MODEL), D_MODEL ** -0.5),
        "b_gate": nrm(ks[5], (DEPTH, N_BRANCH * D_MODEL), 0.01),
        "ssm_a_re": -0.5 + nrm(ks[6], (DEPTH, SSM_GROUPS, SSM_STATE), 0.01),
        "ssm_a_im": math.pi * n_idx + nrm(ks[7], (DEPTH, SSM_GROUPS, SSM_STATE), 0.01),
        "ssm_log_dt": jax.random.uniform(ks[8], (DEPTH, SSM_GROUPS), f32,
                                         math.log(DT_MIN), math.log(DT_MAX)),
        "ssm_b_re": nrm(ks[9], (DEPTH, SSM_GROUPS, SSM_STATE, SSM_GROUP), (2 * SSM_GROUP) ** -0.5),
        "ssm_b_im": nrm(ks[10], (DEPTH, SSM_GROUPS, SSM_STATE, SSM_GROUP), (2 * SSM_GROUP) ** -0.5),
        "ssm_c_re": nrm(ks[11], (DEPTH, SSM_GROUPS, SSM_GROUP, SSM_STATE), SSM_STATE ** -0.5),
        "ssm_c_im": nrm(ks[12], (DEPTH, SSM_GROUPS, SSM_GROUP, SSM_STATE), SSM_STATE ** -0.5),
        "ssm_d": nrm(ks[13], (DEPTH, D_SSM), 1.0),
        "w_glu_val": nrm(ks[14], (DEPTH, D_SSM, D_MODEL), D_SSM ** -0.5),
        "w_glu_gate": nrm(ks[15], (DEPTH, D_SSM, D_MODEL), D_SSM ** -0.5),
        "w_attn_out": nrm(ks[16], (DEPTH, D_SB, D_MODEL), D_SB ** -0.5),
        "conv_w": nrm(ks[17], (DEPTH, CONV_WIDTH, D_CONV), CONV_WIDTH ** -0.5),
        "w_conv_out": nrm(ks[18], (DEPTH, D_CONV, D_MODEL), D_CONV ** -0.5),
        "w_mix_out": nrm(ks[19], (DEPTH, D_MODEL, D_MODEL), D_MODEL ** -0.5),
        "norm_ffn_pre": 1.0 + nrm(ks[20], (DEPTH, D_MODEL), 0.02),
        "norm_ffn_post": 1.0 + nrm(ks[21], (DEPTH, D_MODEL), 0.02),
        "w_ffn_up": nrm(ks[22], (DEPTH, D_MODEL, D_FF), D_MODEL ** -0.5),
        "w_ffn_down": nrm(ks[23], (DEPTH, D_FF, D_MODEL), D_FF ** -0.5),
    }


def reference(x, norm_mix_pre, norm_mix_post, w_in, w_gate, b_gate,
              ssm_a_re, ssm_a_im, ssm_log_dt, ssm_b_re, ssm_b_im, ssm_c_re, ssm_c_im, ssm_d,
              w_glu_val, w_glu_gate, w_attn_out, conv_w, w_conv_out, w_mix_out,
              norm_ffn_pre, norm_ffn_post, w_ffn_up, w_ffn_down):
    bsz, s_len, _ = x.shape

    def heads(t):
        return t.reshape(bsz, s_len, SB_HEADS, SB_HEAD_DIM).transpose(0, 2, 1, 3)

    for l in range(DEPTH):
        h = rmsnorm(x, norm_mix_pre[l])
        p = h @ w_in[l]
        u, q, k, v, cb, cc, cx = jnp.split(p, SPLITS, axis=-1)

        y_ssm = s5_branch(u, ssm_a_re[l], ssm_a_im[l], ssm_log_dt[l], ssm_b_re[l], ssm_b_im[l],
                          ssm_c_re[l], ssm_c_im[l], ssm_d[l], w_glu_val[l], w_glu_gate[l])
        o = stick_breaking_attention(heads(q), heads(k), heads(v))
        y_sb = o.transpose(0, 2, 1, 3).reshape(bsz, s_len, D_SB) @ w_attn_out[l]
        y_conv = short_conv_branch(cb, cc, cx, conv_w[l], w_conv_out[l])

        g_ssm, g_sb, g_conv = jnp.split(jax.nn.sigmoid(h @ w_gate[l] + b_gate[l]), N_BRANCH, axis=-1)
        merged = g_ssm * y_ssm + g_sb * y_sb + g_conv * y_conv
        x = x + rmsnorm(merged @ w_mix_out[l], norm_mix_post[l])

        h = rmsnorm(x, norm_ffn_pre[l])
        f = jnp.square(jax.nn.relu(h @ w_ffn_up[l])) @ w_ffn_down[l]
        x = x + rmsnorm(f, norm_ffn_post[l])
    return x
```

```python
import functools
import math

import jax
import jax.numpy as jnp
from jax import lax
from jax.experimental import pallas as pl
from jax.experimental.pallas import tpu as pltpu

F32 = jnp.float32
BF16 = jnp.bfloat16

EPS = 1e-6
SSM_GROUP = 16
SSM_GROUPS = 16
D_SSM = SSM_GROUP * SSM_GROUPS
SSM_STATE = 64
SB_HEADS = 8
SB_HEAD_DIM = 64
D_SB = SB_HEADS * SB_HEAD_DIM
D_CONV = 256
CONV_WIDTH = 3
N_BRANCH = 3

V7X_LANES = 128
V7X_VMEM_LIMIT = 56 * 1024 * 1024

SSM_CHUNK = 8
SSM_ROWS = 256
TOK_TILE = 512
ATT_BLOCK = 256
HEADS_PER_STEP = V7X_LANES // SB_HEAD_DIM


def _rms(x, g):
    return x * lax.rsqrt(jnp.mean(x * x, axis=-1, keepdims=True) + EPS) * g


def _sigmoid(x):
    return 1.0 / (1.0 + jnp.exp(-x))


def _const_spec(shape):
    return pl.BlockSpec(shape, lambda *_: (0,) * len(shape), pipeline_mode=pl.Buffered(1))


def _proj_kernel(x_ref, g_ref, wm_ref, wvt_ref, cw_ref,
                 u_ref, q_ref, k_ref, vt_ref, cp_ref, z_scr, *, tiles_per_seq):
    tm = x_ref.shape[0]
    h = _rms(x_ref[...], g_ref[...]).astype(BF16)
    p = jnp.dot(h, wm_ref[...], preferred_element_type=F32)
    o = 0
    u_ref[...] = p[:, o:o + D_SSM].astype(BF16); o += D_SSM
    q_ref[...] = (p[:, o:o + D_SB] * (SB_HEAD_DIM ** -0.5)).astype(BF16); o += D_SB
    k_ref[...] = p[:, o:o + D_SB].astype(BF16); o += D_SB
    cb = p[:, o:o + D_CONV]; o += D_CONV
    cc = p[:, o:o + D_CONV]; o += D_CONV
    cx = p[:, o:o + D_CONV]

    vt = lax.dot_general(wvt_ref[...], h, (((1,), (1,)), ((), ())), preferred_element_type=F32)
    for j in range(tm // ATT_BLOCK):
        vt_ref[j] = vt[:, j * ATT_BLOCK:(j + 1) * ATT_BLOCK].astype(BF16)

    @pl.when(pl.program_id(0) % tiles_per_seq == 0)
    def _():
        z_scr[0:8, :] = jnp.zeros((8, D_CONV), F32)

    z = cc * cx
    z_scr[8:8 + tm, :] = z
    z1 = z_scr[7:7 + tm, :]
    z2 = z_scr[6:6 + tm, :]
    cw = cw_ref[...]
    y = cw[0:1, :] * z2 + cw[1:2, :] * z1 + cw[2:3, :] * z
    cp_ref[...] = (cb * y).astype(BF16)
    z_scr[0:8, :] = z_scr[tm:tm + 8, :]


def _proj(x2d, g, wm, wvt, cw, *, seq):
    t, d = x2d.shape
    tm = TOK_TILE
    nb = ATT_BLOCK
    grid = (t // tm,)
    tok = lambda w: pl.BlockSpec((tm, w), lambda i: (i, 0))
    return pl.pallas_call(
        functools.partial(_proj_kernel, tiles_per_seq=seq // tm),
        grid=grid,
        in_specs=[tok(d), _const_spec((1, d)), _const_spec(wm.shape), _const_spec(wvt.shape),
                  _const_spec(cw.shape)],
        out_specs=[tok(D_SSM), tok(D_SB), tok(D_SB),
                   pl.BlockSpec((tm // nb, D_SB, nb), lambda i: (i, 0, 0)),
                   tok(D_CONV)],
        out_shape=[jax.ShapeDtypeStruct((t, D_SSM), BF16),
                   jax.ShapeDtypeStruct((t, D_SB), BF16),
                   jax.ShapeDtypeStruct((t, D_SB), BF16),
                   jax.ShapeDtypeStruct((t // nb, D_SB, nb), BF16),
                   jax.ShapeDtypeStruct((t, D_CONV), BF16)],
        scratch_shapes=[pltpu.VMEM((tm + 8, D_CONV), F32)],
        compiler_params=pltpu.CompilerParams(dimension_semantics=("arbitrary",),
                                             vmem_limit_bytes=V7X_VMEM_LIMIT),
        name="proj",
    )(x2d, g, wm, wvt, cw)


def _s5_kernel(u_ref, bexp_ref, cexp_ref, kt_ref, a_ref, d_ref, y_ref, e_scr, st_scr):
    rows = u_ref.shape[0]
    half = st_scr.shape[1] // 2

    @pl.when(pl.program_id(1) == 0)
    def _():
        st_scr[...] = jnp.zeros(st_scr.shape, F32)

    u = u_ref[...]
    e_scr[...] = jnp.dot(u, bexp_ref[...], preferred_element_type=F32)
    a_re = a_ref[0:1, 0:half]
    a_im = a_ref[0:1, half:]

    def step(j, carry):
        s_re, s_im = carry
        e = e_scr[pl.ds(j, 1), :]
        e_scr[pl.ds(j, 1), :] = jnp.concatenate([s_re, s_im], axis=1)
        n_re = a_re * s_re - a_im * s_im + e[:, 0:half]
        n_im = a_re * s_im + a_im * s_re + e[:, half:]
        return n_re, n_im

    s_re, s_im = lax.fori_loop(0, rows, step, (st_scr[0:1, 0:half], st_scr[0:1, half:]))
    st_scr[0:1, 0:half] = s_re
    st_scr[0:1, half:] = s_im

    y = jnp.dot(e_scr[...].astype(BF16), cexp_ref[...], preferred_element_type=F32)
    y = y + jnp.dot(u, kt_ref[...], preferred_element_type=F32)
    y = y + d_ref[...] * u.astype(F32)
    y_ref[...] = jax.nn.gelu(y).astype(BF16)


def _s5(u8, bexp, cexp, ktoep, a8, d8, *, batch):
    r, w = u8.shape
    rows = SSM_ROWS
    steps = r // batch // rows
    blk = pl.BlockSpec((rows, w), lambda b, i: (b * steps + i, 0))
    return pl.pallas_call(
        _s5_kernel,
        grid=(batch, steps),
        in_specs=[blk, _const_spec(bexp.shape), _const_spec(cexp.shape), _const_spec(ktoep.shape),
                  _const_spec(a8.shape), _const_spec(d8.shape)],
        out_specs=blk,
        out_shape=jax.ShapeDtypeStruct((r, w), BF16),
        scratch_shapes=[pltpu.VMEM((rows, bexp.shape[1]), F32), pltpu.VMEM((8, bexp.shape[1]), F32)],
        compiler_params=pltpu.CompilerParams(dimension_semantics=("arbitrary", "arbitrary"),
                                             vmem_limit_bytes=V7X_VMEM_LIMIT),
        name="s5",
    )(u8, bexp, cexp, ktoep, a8, d8)


def _s5_matrices(a_re, a_im, log_dt, b_re, b_im, c_re, c_im, d_skip):
    hp = lax.Precision.HIGHEST
    lc = SSM_CHUNK
    dt = jnp.exp(log_dt)[:, None]
    mag = jnp.exp(a_re * dt)
    ab_re = mag * jnp.cos(a_im * dt)
    ab_im = mag * jnp.sin(a_im * dt)
    den = a_re * a_re + a_im * a_im
    xr = ab_re - 1.0
    coef_re = (xr * a_re + ab_im * a_im) / den
    coef_im = (ab_im * a_re - xr * a_im) / den
    bb_re = coef_re[..., None] * b_re - coef_im[..., None] * b_im
    bb_im = coef_re[..., None] * b_im + coef_im[..., None] * b_re
    p_re = [jnp.ones_like(ab_re)]
    p_im = [jnp.zeros_like(ab_im)]
    for _ in range(lc):
        r, i = p_re[-1], p_im[-1]
        p_re.append(r * ab_re - i * ab_im)
        p_im.append(r * ab_im + i * ab_re)
    p_re = jnp.stack(p_re)
    p_im = jnp.stack(p_im)
    eye = jnp.eye(SSM_GROUPS, dtype=F32)
    gn = SSM_GROUPS * SSM_STATE
    lw = lc * D_SSM

    pr = p_re[lc - 1::-1][:lc]
    pi = p_im[lc - 1::-1][:lc]
    t_re = pr[:, :, :, None] * bb_re[None] - pi[:, :, :, None] * bb_im[None]
    t_im = pr[:, :, :, None] * bb_im[None] + pi[:, :, :, None] * bb_re[None]
    t1 = jnp.stack([t_re, t_im], axis=0)
    bexp = jnp.einsum('rsgnc,gh->sgcrhn', t1, eye).reshape(lw, 2 * gn)

    qr = p_re[1:lc + 1]
    qi = p_im[1:lc + 1]
    ca_re = c_re[None] * qr[:, :, None, :] - c_im[None] * qi[:, :, None, :]
    ca_im = c_re[None] * qi[:, :, None, :] + c_im[None] * qr[:, :, None, :]
    t2 = jnp.stack([ca_re, -ca_im], axis=0)
    cexp = jnp.einsum('rigcn,gh->rgnihc', t2, eye).reshape(2 * gn, lw)

    cp_re = c_re[None] * p_re[:lc, :, None, :] - c_im[None] * p_im[:lc, :, None, :]
    cp_im = c_re[None] * p_im[:lc, :, None, :] + c_im[None] * p_re[:lc, :, None, :]
    ktap = (jnp.einsum('tgcn,gnd->tgcd', cp_re, bb_re, precision=hp)
            - jnp.einsum('tgcn,gnd->tgcd', cp_im, bb_im, precision=hp))
    pos = jnp.arange(lc)
    tau = pos[None, :] - pos[:, None]
    taps = jnp.where((tau >= 0)[:, :, None, None, None], ktap[jnp.clip(tau, 0, lc - 1)], 0.0)
    ktoep = jnp.einsum('sigcd,gh->sgdihc', taps, eye).reshape(lw, lw)

    a8 = jnp.concatenate([p_re[lc].reshape(1, gn), p_im[lc].reshape(1, gn)], axis=1)
    d8 = jnp.tile(d_skip.reshape(1, D_SSM), (1, lc))
    return bexp.astype(BF16), cexp.astype(BF16), ktoep.astype(BF16), a8, d8


def _softplus(z):
    return jnp.maximum(z, 0.0) + jnp.log(1.0 + jnp.exp(-jnp.abs(z)))


def _attn_kernel(q_ref, k_ref, vt_ref, o_ref):
    bq = q_ref.shape[0]
    bk = bq
    i = pl.program_id(2)
    q = q_ref[...]
    lane = lax.broadcasted_iota(jnp.int32, q.shape, 1)
    key_pos = lax.broadcasted_iota(jnp.int32, (bk, bq), 0)
    qry_pos = lax.broadcasted_iota(jnp.int32, (bk, bq), 1)
    causal = key_pos < qry_pos
    suffix = (lax.broadcasted_iota(jnp.int32, (bk, bk), 1)
              > lax.broadcasted_iota(jnp.int32, (bk, bk), 0)).astype(BF16)

    for hh in range(HEADS_PER_STEP):
        qm = jnp.where(lane // SB_HEAD_DIM == hh, q, jnp.zeros_like(q))
        row0 = hh * SB_HEAD_DIM

        def tile(kb, carry, masked):
            run, acc = carry
            kblk = k_ref[pl.ds(pl.multiple_of(kb * bk, bk), bk), :]
            z = lax.dot_general(kblk, qm, (((1,), (1,)), ((), ())), preferred_element_type=F32)
            sp = _softplus(z)
            if masked:
                sp = jnp.where(causal, sp, 0.0)
            hi = sp.astype(BF16)
            lo = (sp - hi.astype(F32)).astype(BF16)
            c = (jnp.dot(suffix, hi, preferred_element_type=F32)
                 + jnp.dot(suffix, lo, preferred_element_type=F32))
            w = jnp.exp(z - sp - c - run)
            if masked:
                w = jnp.where(causal, w, 0.0)
            vblk = vt_ref[kb, row0:row0 + SB_HEAD_DIM, :]
            acc = acc + jnp.dot(vblk, w.astype(BF16), preferred_element_type=F32)
            run = run + c[0:1, :] + sp[0:1, :]
            return run, acc

        carry = (jnp.zeros((1, bq), F32), jnp.zeros((SB_HEAD_DIM, bq), F32))
        carry = tile(i, carry, True)
        carry = lax.fori_loop(0, i, lambda j, cr: tile(i - 1 - j, cr, False), carry)
        o_ref[row0:row0 + SB_HEAD_DIM, :] = carry[1].astype(BF16)


def _attn(q, k, vt, *, batch, seq):
    bq = ATT_BLOCK
    nq = seq // bq
    pairs = SB_HEADS // HEADS_PER_STEP
    rows = HEADS_PER_STEP * SB_HEAD_DIM
    vt4 = vt.reshape(batch, nq, D_SB, bq)
    return pl.pallas_call(
        _attn_kernel,
        grid=(batch, pairs, nq),
        in_specs=[pl.BlockSpec((bq, rows), lambda b, p, i: (b * nq + i, p)),
                  pl.BlockSpec((seq, rows), lambda b, p, i: (b, p)),
                  pl.BlockSpec((None, nq, rows, bq), lambda b, p, i: (b, 0, p, 0))],
        out_specs=pl.BlockSpec((None, rows, bq), lambda b, p, i: (b, p, i)),
        out_shape=jax.ShapeDtypeStruct((batch, D_SB, seq), BF16),
        compiler_params=pltpu.CompilerParams(
            dimension_semantics=("arbitrary", "arbitrary", "arbitrary"),
            vmem_limit_bytes=V7X_VMEM_LIMIT),
        name="attn",
    )(q, k, vt4)


def _merge_kernel(x_ref, ys_ref, ot_ref, cp_ref, gpre_ref, wg_ref, bg_ref, wval_ref, wglu_ref,
                  wao_ref, wco_ref, wmo_ref, gpost_ref, o_ref):
    d = x_ref.shape[1]
    x = x_ref[...]
    h = _rms(x, gpre_ref[...]).astype(BF16)
    gates = _sigmoid(jnp.dot(h, wg_ref[...], preferred_element_type=F32) + bg_ref[...])
    ys = ys_ref[...]
    y_ssm = (jnp.dot(ys, wval_ref[...], preferred_element_type=F32)
             * _sigmoid(jnp.dot(ys, wglu_ref[...], preferred_element_type=F32)))
    y_sb = lax.dot_general(ot_ref[...], wao_ref[...], (((0,), (0,)), ((), ())),
                           preferred_element_type=F32)
    y_conv = jnp.dot(cp_ref[...], wco_ref[...], preferred_element_type=F32)
    merged = gates[:, 0:d] * y_ssm + gates[:, d:2 * d] * y_sb + gates[:, 2 * d:] * y_conv
    m = jnp.dot(merged.astype(BF16), wmo_ref[...], preferred_element_type=F32)
    o_ref[...] = x + _rms(m, gpost_ref[...])


def _merge(x2d, ys, ot, cp, gpre, wg, bg, wval, wglu, wao, wco, wmo, gpost, *, seq):
    t, d = x2d.shape
    tm = TOK_TILE
    per_seq = seq // tm
    tok = lambda w: pl.BlockSpec((tm, w), lambda i: (i, 0))
    consts = [gpre, wg, bg, wval, wglu, wao, wco, wmo, gpost]
    return pl.pallas_call(
        _merge_kernel,
        grid=(t // tm,),
        in_specs=[tok(d), tok(D_SSM),
                  pl.BlockSpec((None, D_SB, tm), lambda i: (i // per_seq, 0, i % per_seq)),
                  tok(D_CONV)] + [_const_spec(c.shape) for c in consts],
        out_specs=tok(d),
        out_shape=jax.ShapeDtypeStruct((t, d), F32),
        compiler_params=pltpu.CompilerParams(dimension_semantics=("arbitrary",),
                                             vmem_limit_bytes=V7X_VMEM_LIMIT),
        name="merge",
    )(x2d, ys, ot, cp, *consts)


def _ffn_kernel(x_ref, gpre_ref, wup_ref, wdn_ref, gpost_ref, o_ref):
    x = x_ref[...]
    h = _rms(x, gpre_ref[...]).astype(BF16)
    a = jnp.maximum(jnp.dot(h, wup_ref[...], preferred_element_type=F32), 0.0)
    f = jnp.dot((a * a).astype(BF16), wdn_ref[...], preferred_element_type=F32)
    o_ref[...] = x + _rms(f, gpost_ref[...])


def _ffn(x2d, gpre, wup, wdn, gpost):
    t, d = x2d.shape
    tm = TOK_TILE
    tok = pl.BlockSpec((tm, d), lambda i: (i, 0))
    consts = [gpre, wup, wdn, gpost]
    return pl.pallas_call(
        _ffn_kernel,
        grid=(t // tm,),
        in_specs=[tok] + [_const_spec(c.shape) for c in consts],
        out_specs=tok,
        out_shape=jax.ShapeDtypeStruct((t, d), F32),
        compiler_params=pltpu.CompilerParams(dimension_semantics=("arbitrary",),
                                             vmem_limit_bytes=V7X_VMEM_LIMIT),
        name="ffn",
    )(x2d, *consts)


def kernel(x, norm_mix_pre, norm_mix_post, w_in, w_gate, b_gate, ssm_a_re, ssm_a_im, ssm_log_dt,
           ssm_b_re, ssm_b_im, ssm_c_re, ssm_c_im, ssm_d, w_glu_val, w_glu_gate, w_attn_out, conv_w,
           w_conv_out, w_mix_out, norm_ffn_pre, norm_ffn_post, w_ffn_up, w_ffn_down):
    batch, seq, d = x.shape
    depth = w_in.shape[0]
    assert seq % (SSM_CHUNK * SSM_ROWS) == 0 and seq % TOK_TILE == 0 and TOK_TILE % ATT_BLOCK == 0
    t = batch * seq
    v0 = D_SSM + 2 * D_SB
    row = lambda a: a.reshape(1, -1).astype(F32)
    x2d = x.reshape(t, d)
    for l in range(depth):
        w_main = jnp.concatenate([w_in[l][:, :v0], w_in[l][:, v0 + D_SB:]], axis=1).astype(BF16)
        w_vt = w_in[l][:, v0:v0 + D_SB].T.astype(BF16)
        u, q, k, vt, cp = _proj(x2d, row(norm_mix_pre[l]), w_main, w_vt, conv_w[l].astype(F32), seq=seq)

        mats = _s5_matrices(ssm_a_re[l], ssm_a_im[l], ssm_log_dt[l], ssm_b_re[l], ssm_b_im[l],
                            ssm_c_re[l], ssm_c_im[l], ssm_d[l])
        ys = _s5(u.reshape(t // SSM_CHUNK, SSM_CHUNK * D_SSM), *mats, batch=batch).reshape(t, D_SSM)

        ot = _attn(q, k, vt, batch=batch, seq=seq)

        x2d = _merge(x2d, ys, ot, cp, row(norm_mix_pre[l]), w_gate[l].astype(BF16), row(b_gate[l]),
                     w_glu_val[l].astype(BF16), w_glu_gate[l].astype(BF16), w_attn_out[l].astype(BF16),
                     w_conv_out[l].astype(BF16), w_mix_out[l].astype(BF16), row(norm_mix_post[l]), seq=seq)
        x2d = _ffn(x2d, row(norm_ffn_pre[l]), w_ffn_up[l].astype(BF16), w_ffn_down[l].astype(BF16),
                   row(norm_ffn_post[l]))
    return x2d.reshape(batch, seq, d)
```

```python
import functools

import jax
import jax.numpy as jnp
from jax import lax
from jax.experimental import pallas as pl
from jax.experimental.pallas import tpu as pltpu

F32 = jnp.float32
BF16 = jnp.bfloat16

EPS = 1e-6
SSM_GROUP = 16
SSM_GROUPS = 16
D_SSM = SSM_GROUP * SSM_GROUPS
SSM_STATE = 64
SB_HEADS = 8
SB_HEAD_DIM = 64
D_SB = SB_HEADS * SB_HEAD_DIM
D_CONV = 256

V7X_LANES = 128
V7X_VMEM_LIMIT = 56 * 1024 * 1024

SSM_CHUNK = 8
SSM_ROWS = 256
TOK_TILE = 512
ATT_BLOCK = 256
HEADS_PER_STEP = V7X_LANES // SB_HEAD_DIM
F32_EXP_UNDERFLOW = 104.0
NO_TILE_BIAS = 1e30


def _rms(x, g):
    return x * lax.rsqrt(jnp.mean(x * x, axis=-1, keepdims=True) + EPS) * g


def _sigmoid(x):
    return 1.0 / (1.0 + jnp.exp(-x))


def _const_spec(shape):
    return pl.BlockSpec(shape, lambda *_: (0,) * len(shape), pipeline_mode=pl.Buffered(1))


def _proj_kernel(x_ref, g_ref, win_ref, cw_ref, u_ref, q_ref, k_ref, v_ref, cp_ref, z_scr,
                 *, tiles_per_seq):
    tm = x_ref.shape[0]
    h = _rms(x_ref[...], g_ref[...]).astype(BF16)
    p = jnp.dot(h, win_ref[...], preferred_element_type=F32)
    o = 0
    u_ref[...] = p[:, o:o + D_SSM].astype(BF16); o += D_SSM
    q_ref[...] = (p[:, o:o + D_SB] * (SB_HEAD_DIM ** -0.5)).astype(BF16); o += D_SB
    k_ref[...] = p[:, o:o + D_SB].astype(BF16); o += D_SB
    v_ref[...] = p[:, o:o + D_SB].astype(BF16); o += D_SB
    cb = p[:, o:o + D_CONV]; o += D_CONV
    cc = p[:, o:o + D_CONV]; o += D_CONV
    cx = p[:, o:o + D_CONV]

    @pl.when(pl.program_id(0) % tiles_per_seq == 0)
    def _():
        z_scr[0:8, :] = jnp.zeros((8, D_CONV), F32)

    z = cc * cx
    z_scr[8:8 + tm, :] = z
    z1 = z_scr[7:7 + tm, :]
    z2 = z_scr[6:6 + tm, :]
    cw = cw_ref[...]
    y = cw[0:1, :] * z2 + cw[1:2, :] * z1 + cw[2:3, :] * z
    cp_ref[...] = (cb * y).astype(BF16)
    z_scr[0:8, :] = z_scr[tm:tm + 8, :]


def _proj(x2d, g, win, cw, *, seq):
    t, d = x2d.shape
    tm = TOK_TILE
    tok = lambda w: pl.BlockSpec((tm, w), lambda i: (i, 0))
    widths = [D_SSM, D_SB, D_SB, D_SB, D_CONV]
    return pl.pallas_call(
        functools.partial(_proj_kernel, tiles_per_seq=seq // tm),
        grid=(t // tm,),
        in_specs=[tok(d), _const_spec((1, d)), _const_spec(win.shape), _const_spec(cw.shape)],
        out_specs=[tok(w) for w in widths],
        out_shape=[jax.ShapeDtypeStruct((t, w), BF16) for w in widths],
        scratch_shapes=[pltpu.VMEM((tm + 8, D_CONV), F32)],
        compiler_params=pltpu.CompilerParams(dimension_semantics=("arbitrary",),
                                             vmem_limit_bytes=V7X_VMEM_LIMIT),
        name="proj",
    )(x2d, g, win, cw)


def _s5_kernel(u_ref, bexp_ref, cexp_ref, kt_ref, a_ref, d_ref, y_ref, e_scr, st_scr):
    rows = u_ref.shape[0]
    half = st_scr.shape[1] // 2

    @pl.when(pl.program_id(1) == 0)
    def _():
        st_scr[...] = jnp.zeros(st_scr.shape, F32)

    u = u_ref[...]
    e_scr[...] = jnp.dot(u, bexp_ref[...], preferred_element_type=F32)
    a_re = a_ref[0:1, 0:half]
    a_im = a_ref[0:1, half:]

    def step(j, carry):
        s_re, s_im = carry
        e = e_scr[pl.ds(j, 1), :]
        e_scr[pl.ds(j, 1), :] = jnp.concatenate([s_re, s_im], axis=1)
        n_re = a_re * s_re - a_im * s_im + e[:, 0:half]
        n_im = a_re * s_im + a_im * s_re + e[:, half:]
        return n_re, n_im

    s_re, s_im = lax.fori_loop(0, rows, step, (st_scr[0:1, 0:half], st_scr[0:1, half:]))
    st_scr[0:1, 0:half] = s_re
    st_scr[0:1, half:] = s_im

    y = jnp.dot(e_scr[...].astype(BF16), cexp_ref[...], preferred_element_type=F32)
    y = y + jnp.dot(u, kt_ref[...], preferred_element_type=F32)
    y = y + d_ref[...] * u.astype(F32)
    y_ref[...] = jax.nn.gelu(y).astype(BF16)


def _s5(u8, bexp, cexp, ktoep, a8, d8, *, batch):
    r, w = u8.shape
    rows = SSM_ROWS
    steps = r // batch // rows
    blk = pl.BlockSpec((rows, w), lambda b, i: (b * steps + i, 0))
    return pl.pallas_call(
        _s5_kernel,
        grid=(batch, steps),
        in_specs=[blk, _const_spec(bexp.shape), _const_spec(cexp.shape), _const_spec(ktoep.shape),
                  _const_spec(a8.shape), _const_spec(d8.shape)],
        out_specs=blk,
        out_shape=jax.ShapeDtypeStruct((r, w), BF16),
        scratch_shapes=[pltpu.VMEM((rows, bexp.shape[1]), F32), pltpu.VMEM((8, bexp.shape[1]), F32)],
        compiler_params=pltpu.CompilerParams(dimension_semantics=("arbitrary", "arbitrary"),
                                             vmem_limit_bytes=V7X_VMEM_LIMIT),
        name="s5",
    )(u8, bexp, cexp, ktoep, a8, d8)


def _s5_matrices(a_re, a_im, log_dt, b_re, b_im, c_re, c_im, d_skip):
    hp = lax.Precision.HIGHEST
    lc = SSM_CHUNK
    dt = jnp.exp(log_dt)[:, None]
    mag = jnp.exp(a_re * dt)
    ab_re = mag * jnp.cos(a_im * dt)
    ab_im = mag * jnp.sin(a_im * dt)
    den = a_re * a_re + a_im * a_im
    xr = ab_re - 1.0
    coef_re = (xr * a_re + ab_im * a_im) / den
    coef_im = (ab_im * a_re - xr * a_im) / den
    bb_re = coef_re[..., None] * b_re - coef_im[..., None] * b_im
    bb_im = coef_re[..., None] * b_im + coef_im[..., None] * b_re
    p_re = [jnp.ones_like(ab_re)]
    p_im = [jnp.zeros_like(ab_im)]
    for _ in range(lc):
        r, i = p_re[-1], p_im[-1]
        p_re.append(r * ab_re - i * ab_im)
        p_im.append(r * ab_im + i * ab_re)
    p_re = jnp.stack(p_re)
    p_im = jnp.stack(p_im)
    eye = jnp.eye(SSM_GROUPS, dtype=F32)
    gn = SSM_GROUPS * SSM_STATE
    lw = lc * D_SSM

    pr = p_re[lc - 1::-1][:lc]
    pi = p_im[lc - 1::-1][:lc]
    t_re = pr[:, :, :, None] * bb_re[None] - pi[:, :, :, None] * bb_im[None]
    t_im = pr[:, :, :, None] * bb_im[None] + pi[:, :, :, None] * bb_re[None]
    t1 = jnp.stack([t_re, t_im], axis=0)
    bexp = jnp.einsum('rsgnc,gh->sgcrhn', t1, eye).reshape(lw, 2 * gn)

    qr = p_re[1:lc + 1]
    qi = p_im[1:lc + 1]
    ca_re = c_re[None] * qr[:, :, None, :] - c_im[None] * qi[:, :, None, :]
    ca_im = c_re[None] * qi[:, :, None, :] + c_im[None] * qr[:, :, None, :]
    t2 = jnp.stack([ca_re, -ca_im], axis=0)
    cexp = jnp.einsum('rigcn,gh->rgnihc', t2, eye).reshape(2 * gn, lw)

    cp_re = c_re[None] * p_re[:lc, :, None, :] - c_im[None] * p_im[:lc, :, None, :]
    cp_im = c_re[None] * p_im[:lc, :, None, :] + c_im[None] * p_re[:lc, :, None, :]
    ktap = (jnp.einsum('tgcn,gnd->tgcd', cp_re, bb_re, precision=hp)
            - jnp.einsum('tgcn,gnd->tgcd', cp_im, bb_im, precision=hp))
    pos = jnp.arange(lc)
    tau = pos[None, :] - pos[:, None]
    taps = jnp.where((tau >= 0)[:, :, None, None, None], ktap[jnp.clip(tau, 0, lc - 1)], 0.0)
    ktoep = jnp.einsum('sigcd,gh->sgdihc', taps, eye).reshape(lw, lw)

    a8 = jnp.concatenate([p_re[lc].reshape(1, gn), p_im[lc].reshape(1, gn)], axis=1)
    d8 = jnp.tile(d_skip.reshape(1, D_SSM), (1, lc))
    return bexp.astype(BF16), cexp.astype(BF16), ktoep.astype(BF16), a8, d8


def _softplus(z):
    return jnp.maximum(z, 0.0) + jnp.log(1.0 + jnp.exp(-jnp.abs(z)))


def _attn_kernel(q_ref, k_ref, v_ref, o_ref):
    bq, lanes = q_ref.shape
    bk = bq
    i = pl.program_id(2)
    q = q_ref[...]
    lane = lax.broadcasted_iota(jnp.int32, (bq, lanes), 1)
    causal = (lax.broadcasted_iota(jnp.int32, (bq, bk), 1)
              < lax.broadcasted_iota(jnp.int32, (bq, bk), 0))
    key_j = lax.broadcasted_iota(jnp.int32, (2 * bk, bk), 0) % bk
    suffix = (key_j > lax.broadcasted_iota(jnp.int32, (2 * bk, bk), 1)).astype(BF16)
    heads = range(HEADS_PER_STEP)
    qm = [jnp.where(lane // SB_HEAD_DIM == hh, q, jnp.zeros_like(q)) for hh in heads]

    def tile(hh, kb, run, acc, masked):
        rows = pl.ds(pl.multiple_of(kb * bk, bk), bk)
        z = lax.dot_general(qm[hh], k_ref[rows, :], (((1,), (1,)), ((), ())),
                            preferred_element_type=F32)
        sp = _softplus(z)
        if masked:
            sp = jnp.where(causal, sp, 0.0)
        hi = sp.astype(BF16)
        lo = (sp - hi.astype(F32)).astype(BF16)
        c = jnp.dot(jnp.concatenate([hi, lo], axis=1), suffix, preferred_element_type=F32)
        w = jnp.exp(z - sp - c - jnp.concatenate([run] * (bk // lanes), axis=1))
        if masked:
            w = jnp.where(causal, w, 0.0)
        acc = acc + jnp.dot(w.astype(BF16), v_ref[rows, :], preferred_element_type=F32)
        return run + jnp.broadcast_to(c[:, 0:1] + sp[:, 0:1], run.shape), acc

    run = [jnp.zeros((bq, lanes), F32) for _ in heads]
    acc = [jnp.zeros((bq, lanes), F32) for _ in heads]
    for hh in heads:
        run[hh], acc[hh] = tile(hh, i, run[hh], acc[hh], True)
    left = jnp.maximum(i - 1, 0)
    bias = jnp.where(i > 0, 0.0, NO_TILE_BIAS).astype(F32)
    for hh in heads:
        run[hh], acc[hh] = tile(hh, left, run[hh] + bias, acc[hh], False)

    def more(st):
        j, runs, _ = st
        low = functools.reduce(jnp.minimum, [jnp.min(r) for r in runs])
        return jnp.logical_and(j <= i, low < F32_EXP_UNDERFLOW)

    def step(st):
        j, runs, accs = st
        out = [tile(hh, i - j, runs[hh], accs[hh], False) for hh in heads]
        return j + 1, tuple(o[0] for o in out), tuple(o[1] for o in out)

    _, _, acc = lax.while_loop(more, step, (jnp.int32(2), tuple(run), tuple(acc)))
    o_ref[...] = jnp.where(lane < SB_HEAD_DIM, acc[0], acc[1]).astype(BF16)


def _attn(q, k, v, *, batch, seq):
    bq = ATT_BLOCK
    nq = seq // bq
    assert HEADS_PER_STEP == 2
    return pl.pallas_call(
        _attn_kernel,
        grid=(batch, D_SB // V7X_LANES, nq),
        in_specs=[pl.BlockSpec((bq, V7X_LANES), lambda b, p, i: (b * nq + i, p)),
                  pl.BlockSpec((seq, V7X_LANES), lambda b, p, i: (b, p)),
                  pl.BlockSpec((seq, V7X_LANES), lambda b, p, i: (b, p))],
        out_specs=pl.BlockSpec((bq, V7X_LANES), lambda b, p, i: (b * nq + i, p)),
        out_shape=jax.ShapeDtypeStruct((batch * seq, D_SB), BF16),
        compiler_params=pltpu.CompilerParams(
            dimension_semantics=("arbitrary", "arbitrary", "arbitrary"),
            vmem_limit_bytes=V7X_VMEM_LIMIT),
        name="attn",
    )(q, k, v)


def _merge_kernel(x_ref, ys_ref, oa_ref, cp_ref, gpre_ref, wg_ref, bg_ref, wval_ref, wglu_ref,
                  wao_ref, wco_ref, wmo_ref, gpost_ref, o_ref):
    d = x_ref.shape[1]
    x = x_ref[...]
    h = _rms(x, gpre_ref[...]).astype(BF16)
    gates = _sigmoid(jnp.dot(h, wg_ref[...], preferred_element_type=F32) + bg_ref[...])
    ys = ys_ref[...]
    y_ssm = (jnp.dot(ys, wval_ref[...], preferred_element_type=F32)
             * _sigmoid(jnp.dot(ys, wglu_ref[...], preferred_element_type=F32)))
    y_sb = jnp.dot(oa_ref[...], wao_ref[...], preferred_element_type=F32)
    y_conv = jnp.dot(cp_ref[...], wco_ref[...], preferred_element_type=F32)
    merged = gates[:, 0:d] * y_ssm + gates[:, d:2 * d] * y_sb + gates[:, 2 * d:] * y_conv
    m = jnp.dot(merged.astype(BF16), wmo_ref[...], preferred_element_type=F32)
    o_ref[...] = x + _rms(m, gpost_ref[...])


def _merge(x2d, ys, oa, cp, gpre, wg, bg, wval, wglu, wao, wco, wmo, gpost):
    t, d = x2d.shape
    tm = TOK_TILE
    tok = lambda w: pl.BlockSpec((tm, w), lambda i: (i, 0))
    consts = [gpre, wg, bg, wval, wglu, wao, wco, wmo, gpost]
    return pl.pallas_call(
        _merge_kernel,
        grid=(t // tm,),
        in_specs=[tok(d), tok(D_SSM), tok(D_SB), tok(D_CONV)] + [_const_spec(c.shape) for c in consts],
        out_specs=tok(d),
        out_shape=jax.ShapeDtypeStruct((t, d), F32),
        compiler_params=pltpu.CompilerParams(dimension_semantics=("arbitrary",),
                                             vmem_limit_bytes=V7X_VMEM_LIMIT),
        name="merge",
    )(x2d, ys, oa, cp, *consts)


def _ffn_kernel(x_ref, gpre_ref, wup_ref, wdn_ref, gpost_ref, o_ref):
    x = x_ref[...]
    h = _rms(x, gpre_ref[...]).astype(BF16)
    a = jnp.maximum(jnp.dot(h, wup_ref[...], preferred_element_type=F32), 0.0)
    f = jnp.dot((a * a).astype(BF16), wdn_ref[...], preferred_element_type=F32)
    o_ref[...] = x + _rms(f, gpost_ref[...])


def _ffn(x2d, gpre, wup, wdn, gpost):
    t, d = x2d.shape
    tm = TOK_TILE
    tok = pl.BlockSpec((tm, d), lambda i: (i, 0))
    consts = [gpre, wup, wdn, gpost]
    return pl.pallas_call(
        _ffn_kernel,
        grid=(t // tm,),
        in_specs=[tok] + [_const_spec(c.shape) for c in consts],
        out_specs=tok,
        out_shape=jax.ShapeDtypeStruct((t, d), F32),
        compiler_params=pltpu.CompilerParams(dimension_semantics=("arbitrary",),
                                             vmem_limit_bytes=V7X_VMEM_LIMIT),
        name="ffn",
    )(x2d, *consts)


def kernel(x, norm_mix_pre, norm_mix_post, w_in, w_gate, b_gate, ssm_a_re, ssm_a_im, ssm_log_dt,
           ssm_b_re, ssm_b_im, ssm_c_re, ssm_c_im, ssm_d, w_glu_val, w_glu_gate, w_attn_out, conv_w,
           w_conv_out, w_mix_out, norm_ffn_pre, norm_ffn_post, w_ffn_up, w_ffn_down):
    batch, seq, d = x.shape
    depth = w_in.shape[0]
    assert seq % (SSM_CHUNK * SSM_ROWS) == 0 and seq % TOK_TILE == 0 and seq % ATT_BLOCK == 0
    t = batch * seq
    row = lambda a: a.reshape(1, -1).astype(F32)
    x2d = x.reshape(t, d)
    for l in range(depth):
        u, q, k, v, cp = _proj(x2d, row(norm_mix_pre[l]), w_in[l].astype(BF16), conv_w[l].astype(F32),
                               seq=seq)

        mats = _s5_matrices(ssm_a_re[l], ssm_a_im[l], ssm_log_dt[l], ssm_b_re[l], ssm_b_im[l],
                            ssm_c_re[l], ssm_c_im[l], ssm_d[l])
        ys = _s5(u.reshape(t // SSM_CHUNK, SSM_CHUNK * D_SSM), *mats, batch=batch).reshape(t, D_SSM)

        oa = _attn(q, k, v, batch=batch, seq=seq)

        x2d = _merge(x2d, ys, oa, cp, row(norm_mix_pre[l]), w_gate[l].astype(BF16), row(b_gate[l]),
                     w_glu_val[l].astype(BF16), w_glu_gate[l].astype(BF16), w_attn_out[l].astype(BF16),
                     w_conv_out[l].astype(BF16), w_mix_out[l].astype(BF16), row(norm_mix_post[l]))
        x2d = _ffn(x2d, row(norm_ffn_pre[l]), w_ffn_up[l].astype(BF16), w_ffn_down[l].astype(BF16),
                   row(norm_ffn_post[l]))
    return x2d.reshape(batch, seq, d)
```

```python
import functools

import jax
import jax.numpy as jnp
from jax import lax
from jax.experimental import pallas as pl
from jax.experimental.pallas import tpu as pltpu

F32 = jnp.float32
BF16 = jnp.bfloat16

EPS = 1e-6
SSM_GROUP = 16
SSM_GROUPS = 16
D_SSM = SSM_GROUP * SSM_GROUPS
SSM_STATE = 64
SB_HEADS = 8
SB_HEAD_DIM = 64
D_SB = SB_HEADS * SB_HEAD_DIM
D_CONV = 256

V7X_LANES = 128
V7X_VMEM_LIMIT = 56 * 1024 * 1024

SSM_CHUNK = 8
SSM_ROWS = 256
TOK_TILE = 512
ATT_BLOCK = 256
HEADS_PER_STEP = V7X_LANES // SB_HEAD_DIM
ATT_GROUPS = 2
F32_EXP_UNDERFLOW = 104.0
NO_TILE_BIAS = 1e30


def _rms(x, g):
    return x * lax.rsqrt(jnp.mean(x * x, axis=-1, keepdims=True) + EPS) * g


def _sigmoid(x):
    return 1.0 / (1.0 + jnp.exp(-x))


def _const_spec(shape):
    return pl.BlockSpec(shape, lambda *_: (0,) * len(shape), pipeline_mode=pl.Buffered(1))


def _proj_kernel(x_ref, g_ref, win_ref, cw_ref, u8_ref, q_ref, k_ref, v_ref, cp_ref, z_scr, u_scr,
                 *, tiles_per_seq):
    tm = x_ref.shape[0]
    h = _rms(x_ref[...], g_ref[...]).astype(BF16)
    p = jnp.dot(h, win_ref[...], preferred_element_type=F32)
    o = 0
    for hf in range(D_SSM // V7X_LANES):
        u_scr[hf] = p[:, hf * V7X_LANES:(hf + 1) * V7X_LANES]
    for pos in range(SSM_CHUNK):
        for hf in range(D_SSM // V7X_LANES):
            lo = pos * D_SSM + hf * V7X_LANES
            u8_ref[:, lo:lo + V7X_LANES] = u_scr[
                hf, pl.ds(pos, tm // SSM_CHUNK, stride=SSM_CHUNK), :].astype(BF16)
    o += D_SSM
    q_ref[...] = (p[:, o:o + D_SB] * (SB_HEAD_DIM ** -0.5)).astype(BF16); o += D_SB
    k_ref[...] = p[:, o:o + D_SB].astype(BF16); o += D_SB
    v_ref[...] = p[:, o:o + D_SB].astype(BF16); o += D_SB
    cb = p[:, o:o + D_CONV]; o += D_CONV
    cc = p[:, o:o + D_CONV]; o += D_CONV
    cx = p[:, o:o + D_CONV]

    @pl.when(pl.program_id(0) % tiles_per_seq == 0)
    def _():
        z_scr[0:8, :] = jnp.zeros((8, D_CONV), F32)

    z = cc * cx
    z_scr[8:8 + tm, :] = z
    z1 = z_scr[7:7 + tm, :]
    z2 = z_scr[6:6 + tm, :]
    cw = cw_ref[...]
    y = cw[0:1, :] * z2 + cw[1:2, :] * z1 + cw[2:3, :] * z
    cp_ref[...] = (cb * y).astype(BF16)
    z_scr[0:8, :] = z_scr[tm:tm + 8, :]


def _proj(x2d, g, win, cw, *, seq):
    t, d = x2d.shape
    tm = TOK_TILE
    tok = lambda w: pl.BlockSpec((tm, w), lambda i: (i, 0))
    widths = [D_SB, D_SB, D_SB, D_CONV]
    lc = SSM_CHUNK
    return pl.pallas_call(
        functools.partial(_proj_kernel, tiles_per_seq=seq // tm),
        grid=(t // tm,),
        in_specs=[tok(d), _const_spec((1, d)), _const_spec(win.shape), _const_spec(cw.shape)],
        out_specs=[pl.BlockSpec((tm // lc, lc * D_SSM), lambda i: (i, 0))] + [tok(w) for w in widths],
        out_shape=[jax.ShapeDtypeStruct((t // lc, lc * D_SSM), BF16)]
        + [jax.ShapeDtypeStruct((t, w), BF16) for w in widths],
        scratch_shapes=[pltpu.VMEM((tm + 8, D_CONV), F32),
                        pltpu.VMEM((D_SSM // V7X_LANES, tm, V7X_LANES), F32)],
        compiler_params=pltpu.CompilerParams(dimension_semantics=("arbitrary",),
                                             vmem_limit_bytes=V7X_VMEM_LIMIT),
        name="proj",
    )(x2d, g, win, cw)


def _s5_kernel(u_ref, bexp_ref, cexp_ref, kt_ref, a_ref, d_ref, y_ref, e_scr, st_scr, y_scr):
    rows = u_ref.shape[0]
    half = st_scr.shape[1] // 2

    @pl.when(pl.program_id(1) == 0)
    def _():
        st_scr[...] = jnp.zeros(st_scr.shape, F32)

    u = u_ref[...]
    e_scr[...] = jnp.dot(u, bexp_ref[...], preferred_element_type=F32)
    a_re = a_ref[0:1, 0:half]
    a_im = a_ref[0:1, half:]

    def step(j, carry):
        s_re, s_im = carry
        e = e_scr[pl.ds(j, 1), :]
        e_scr[pl.ds(j, 1), :] = jnp.concatenate([s_re, s_im], axis=1)
        n_re = a_re * s_re - a_im * s_im + e[:, 0:half]
        n_im = a_re * s_im + a_im * s_re + e[:, half:]
        return n_re, n_im

    s_re, s_im = lax.fori_loop(0, rows, step, (st_scr[0:1, 0:half], st_scr[0:1, half:]))
    st_scr[0:1, 0:half] = s_re
    st_scr[0:1, half:] = s_im

    y = jnp.dot(e_scr[...].astype(BF16), cexp_ref[...], preferred_element_type=F32)
    y = y + jnp.dot(u, kt_ref[...], preferred_element_type=F32)
    y = y + d_ref[...] * u.astype(F32)
    y = jax.nn.gelu(y)
    for pos in range(SSM_CHUNK):
        for hf in range(D_SSM // V7X_LANES):
            lo = pos * D_SSM + hf * V7X_LANES
            y_scr[hf, pl.ds(pos, rows, stride=SSM_CHUNK), :] = y[:, lo:lo + V7X_LANES]
    for hf in range(D_SSM // V7X_LANES):
        y_ref[:, hf * V7X_LANES:(hf + 1) * V7X_LANES] = y_scr[hf].astype(BF16)


def _s5(u8, bexp, cexp, ktoep, a8, d8, *, batch):
    r, w = u8.shape
    rows = SSM_ROWS
    steps = r // batch // rows
    lc = SSM_CHUNK
    return pl.pallas_call(
        _s5_kernel,
        grid=(batch, steps),
        in_specs=[pl.BlockSpec((rows, w), lambda b, i: (b * steps + i, 0)),
                  _const_spec(bexp.shape), _const_spec(cexp.shape), _const_spec(ktoep.shape),
                  _const_spec(a8.shape), _const_spec(d8.shape)],
        out_specs=pl.BlockSpec((rows * lc, D_SSM), lambda b, i: (b * steps + i, 0)),
        out_shape=jax.ShapeDtypeStruct((r * lc, D_SSM), BF16),
        scratch_shapes=[pltpu.VMEM((rows, bexp.shape[1]), F32), pltpu.VMEM((8, bexp.shape[1]), F32),
                        pltpu.VMEM((D_SSM // V7X_LANES, rows * lc, V7X_LANES), F32)],
        compiler_params=pltpu.CompilerParams(dimension_semantics=("arbitrary", "arbitrary"),
                                             vmem_limit_bytes=V7X_VMEM_LIMIT),
        name="s5",
    )(u8, bexp, cexp, ktoep, a8, d8)


def _s5_matrices(a_re, a_im, log_dt, b_re, b_im, c_re, c_im, d_skip):
    hp = lax.Precision.HIGHEST
    lc = SSM_CHUNK
    dt = jnp.exp(log_dt)[:, None]
    mag = jnp.exp(a_re * dt)
    ab_re = mag * jnp.cos(a_im * dt)
    ab_im = mag * jnp.sin(a_im * dt)
    den = a_re * a_re + a_im * a_im
    xr = ab_re - 1.0
    coef_re = (xr * a_re + ab_im * a_im) / den
    coef_im = (ab_im * a_re - xr * a_im) / den
    bb_re = coef_re[..., None] * b_re - coef_im[..., None] * b_im
    bb_im = coef_re[..., None] * b_im + coef_im[..., None] * b_re
    p_re = [jnp.ones_like(ab_re)]
    p_im = [jnp.zeros_like(ab_im)]
    for _ in range(lc):
        r, i = p_re[-1], p_im[-1]
        p_re.append(r * ab_re - i * ab_im)
        p_im.append(r * ab_im + i * ab_re)
    p_re = jnp.stack(p_re)
    p_im = jnp.stack(p_im)
    gn = SSM_GROUPS * SSM_STATE
    lw = lc * D_SSM

    def expand(compact, row_w, col_w, cols):
        n_in = compact.shape[1]
        src = lax.broadcasted_iota(jnp.int32, (n_in, cols), 0)
        dst = lax.broadcasted_iota(jnp.int32, (n_in, cols), 1)
        per = cols // SSM_GROUPS // col_w
        sel = (src == (dst // (SSM_GROUPS * col_w)) * col_w + dst % col_w).astype(F32)
        assert per * col_w == n_in
        full = jnp.dot(compact, sel, precision=hp)
        rg = lax.broadcasted_iota(jnp.int32, full.shape, 0) // row_w % SSM_GROUPS
        cg = lax.broadcasted_iota(jnp.int32, full.shape, 1) // col_w % SSM_GROUPS
        return jnp.where(rg == cg, full, 0.0).astype(BF16)

    pr = p_re[lc - 1::-1][:lc]
    pi = p_im[lc - 1::-1][:lc]
    t_re = pr[:, :, :, None] * bb_re[None] - pi[:, :, :, None] * bb_im[None]
    t_im = pr[:, :, :, None] * bb_im[None] + pi[:, :, :, None] * bb_re[None]
    t1 = jnp.stack([t_re, t_im], axis=0)
    bexp = expand(t1.transpose(1, 2, 4, 0, 3).reshape(lw, 2 * SSM_STATE), SSM_GROUP, SSM_STATE, 2 * gn)

    qr = p_re[1:lc + 1]
    qi = p_im[1:lc + 1]
    ca_re = c_re[None] * qr[:, :, None, :] - c_im[None] * qi[:, :, None, :]
    ca_im = c_re[None] * qi[:, :, None, :] + c_im[None] * qr[:, :, None, :]
    t2 = jnp.stack([ca_re, -ca_im], axis=0)
    cexp = expand(t2.transpose(0, 2, 4, 1, 3).reshape(2 * gn, lc * SSM_GROUP), SSM_STATE, SSM_GROUP, lw)

    cp_re = c_re[None] * p_re[:lc, :, None, :] - c_im[None] * p_im[:lc, :, None, :]
    cp_im = c_re[None] * p_im[:lc, :, None, :] + c_im[None] * p_re[:lc, :, None, :]
    ktap = (jnp.einsum('tgcn,gnd->tgcd', cp_re, bb_re, precision=hp)
            - jnp.einsum('tgcn,gnd->tgcd', cp_im, bb_im, precision=hp))
    pos = jnp.arange(lc)
    tau = pos[None, :] - pos[:, None]
    taps = jnp.where((tau >= 0)[:, :, None, None, None], ktap[jnp.clip(tau, 0, lc - 1)], 0.0)
    ktoep = expand(taps.transpose(0, 2, 4, 1, 3).reshape(lw, lc * SSM_GROUP), SSM_GROUP, SSM_GROUP, lw)

    a8 = jnp.concatenate([p_re[lc].reshape(1, gn), p_im[lc].reshape(1, gn)], axis=1)
    d8 = jnp.tile(d_skip.reshape(1, D_SSM), (1, lc))
    return bexp.astype(BF16), cexp.astype(BF16), ktoep.astype(BF16), a8, d8


def _softplus(z):
    return jnp.maximum(z, 0.0) + jnp.log(1.0 + jnp.exp(-jnp.abs(z)))


def _attn_kernel(q_ref, k_ref, v_ref, o_ref):
    bq = q_ref.shape[0]
    lanes = V7X_LANES
    bk = bq
    i = pl.program_id(2)
    lane = lax.broadcasted_iota(jnp.int32, (bq, lanes), 1)
    causal = (lax.broadcasted_iota(jnp.int32, (bq, bk), 1)
              < lax.broadcasted_iota(jnp.int32, (bq, bk), 0))
    key_j = lax.broadcasted_iota(jnp.int32, (2 * bk, bk), 0) % bk
    suffix = (key_j > lax.broadcasted_iota(jnp.int32, (2 * bk, bk), 1)).astype(BF16)
    heads = range(ATT_GROUPS * HEADS_PER_STEP)
    cols = [slice(hh // HEADS_PER_STEP * lanes, (hh // HEADS_PER_STEP + 1) * lanes) for hh in heads]
    qm = [jnp.where(lane // SB_HEAD_DIM == hh % HEADS_PER_STEP, q_ref[:, cols[hh]], 0).astype(BF16)
          for hh in heads]

    def tile(hh, kb, run, acc, masked):
        rows = pl.ds(pl.multiple_of(kb * bk, bk), bk)
        z = lax.dot_general(qm[hh], k_ref[rows, cols[hh]], (((1,), (1,)), ((), ())),
                            preferred_element_type=F32)
        sp = _softplus(z)
        if masked:
            sp = jnp.where(causal, sp, 0.0)
        hi = sp.astype(BF16)
        lo = (sp - hi.astype(F32)).astype(BF16)
        c = jnp.dot(jnp.concatenate([hi, lo], axis=1), suffix, preferred_element_type=F32)
        w = jnp.exp(z - sp - c - jnp.concatenate([run] * (bk // lanes), axis=1))
        if masked:
            w = jnp.where(causal, w, 0.0)
        acc = acc + jnp.dot(w.astype(BF16), v_ref[rows, cols[hh]], preferred_element_type=F32)
        return run + jnp.broadcast_to(c[:, 0:1] + sp[:, 0:1], run.shape), acc

    run = [jnp.zeros((bq, lanes), F32) for _ in heads]
    acc = [jnp.zeros((bq, lanes), F32) for _ in heads]
    for hh in heads:
        run[hh], acc[hh] = tile(hh, i, run[hh], acc[hh], True)
    left = jnp.maximum(i - 1, 0)
    bias = jnp.where(i > 0, 0.0, NO_TILE_BIAS).astype(F32)
    for hh in heads:
        run[hh], acc[hh] = tile(hh, left, run[hh] + bias, acc[hh], False)

    def more(st):
        j, runs, _ = st
        low = functools.reduce(jnp.minimum, [jnp.min(r) for r in runs])
        return jnp.logical_and(j <= i, low < F32_EXP_UNDERFLOW)

    def step(st):
        j, runs, accs = st
        out = [tile(hh, i - j, runs[hh], accs[hh], False) for hh in heads]
        return j + 1, tuple(o[0] for o in out), tuple(o[1] for o in out)

    _, _, acc = lax.while_loop(more, step, (jnp.int32(2), tuple(run), tuple(acc)))
    for g in range(ATT_GROUPS):
        o_ref[:, g * lanes:(g + 1) * lanes] = jnp.where(
            lane < SB_HEAD_DIM, acc[g * HEADS_PER_STEP], acc[g * HEADS_PER_STEP + 1]).astype(BF16)


def _attn(q, k, v, *, batch, seq):
    bq = ATT_BLOCK
    nq = seq // bq
    assert HEADS_PER_STEP == 2
    wcols = ATT_GROUPS * V7X_LANES
    return pl.pallas_call(
        _attn_kernel,
        grid=(batch, D_SB // wcols, nq),
        in_specs=[pl.BlockSpec((bq, wcols), lambda b, p, i: (b * nq + i, p)),
                  pl.BlockSpec((seq, wcols), lambda b, p, i: (b, p)),
                  pl.BlockSpec((seq, wcols), lambda b, p, i: (b, p))],
        out_specs=pl.BlockSpec((bq, wcols), lambda b, p, i: (b * nq + i, p)),
        out_shape=jax.ShapeDtypeStruct((batch * seq, D_SB), BF16),
        compiler_params=pltpu.CompilerParams(
            dimension_semantics=("arbitrary", "arbitrary", "arbitrary"),
            vmem_limit_bytes=V7X_VMEM_LIMIT),
        name="attn",
    )(q, k, v)


def _merge_kernel(x_ref, ys_ref, oa_ref, cp_ref, gpre_ref, wg_ref, bg_ref, wval_ref, wglu_ref,
                  wao_ref, wco_ref, wmo_ref, gpost_ref, o_ref):
    d = x_ref.shape[1]
    x = x_ref[...]
    h = _rms(x, gpre_ref[...]).astype(BF16)
    gates = _sigmoid(jnp.dot(h, wg_ref[...], preferred_element_type=F32) + bg_ref[...])
    ys = ys_ref[...]
    y_ssm = (jnp.dot(ys, wval_ref[...], preferred_element_type=F32)
             * _sigmoid(jnp.dot(ys, wglu_ref[...], preferred_element_type=F32)))
    y_sb = jnp.dot(oa_ref[...], wao_ref[...], preferred_element_type=F32)
    y_conv = jnp.dot(cp_ref[...], wco_ref[...], preferred_element_type=F32)
    merged = gates[:, 0:d] * y_ssm + gates[:, d:2 * d] * y_sb + gates[:, 2 * d:] * y_conv
    m = jnp.dot(merged.astype(BF16), wmo_ref[...], preferred_element_type=F32)
    o_ref[...] = x + _rms(m, gpost_ref[...])


def _merge(x2d, ys, oa, cp, gpre, wg, bg, wval, wglu, wao, wco, wmo, gpost):
    t, d = x2d.shape
    tm = TOK_TILE
    tok = lambda w: pl.BlockSpec((tm, w), lambda i: (i, 0))
    consts = [gpre, wg, bg, wval, wglu, wao, wco, wmo, gpost]
    return pl.pallas_call(
        _merge_kernel,
        grid=(t // tm,),
        in_specs=[tok(d), tok(D_SSM), tok(D_SB), tok(D_CONV)] + [_const_spec(c.shape) for c in consts],
        out_specs=tok(d),
        out_shape=jax.ShapeDtypeStruct((t, d), F32),
        compiler_params=pltpu.CompilerParams(dimension_semantics=("arbitrary",),
                                             vmem_limit_bytes=V7X_VMEM_LIMIT),
        name="merge",
    )(x2d, ys, oa, cp, *consts)


def _ffn_kernel(x_ref, gpre_ref, wup_ref, wdn_ref, gpost_ref, o_ref):
    x = x_ref[...]
    h = _rms(x, gpre_ref[...]).astype(BF16)
    a = jnp.maximum(jnp.dot(h, wup_ref[...], preferred_element_type=F32), 0.0)
    f = jnp.dot((a * a).astype(BF16), wdn_ref[...], preferred_element_type=F32)
    o_ref[...] = x + _rms(f, gpost_ref[...])


def _ffn(x2d, gpre, wup, wdn, gpost):
    t, d = x2d.shape
    tm = TOK_TILE
    tok = pl.BlockSpec((tm, d), lambda i: (i, 0))
    consts = [gpre, wup, wdn, gpost]
    return pl.pallas_call(
        _ffn_kernel,
        grid=(t // tm,),
        in_specs=[tok] + [_const_spec(c.shape) for c in consts],
        out_specs=tok,
        out_shape=jax.ShapeDtypeStruct((t, d), F32),
        compiler_params=pltpu.CompilerParams(dimension_semantics=("arbitrary",),
                                             vmem_limit_bytes=V7X_VMEM_LIMIT),
        name="ffn",
    )(x2d, *consts)


def kernel(x, norm_mix_pre, norm_mix_post, w_in, w_gate, b_gate, ssm_a_re, ssm_a_im, ssm_log_dt,
           ssm_b_re, ssm_b_im, ssm_c_re, ssm_c_im, ssm_d, w_glu_val, w_glu_gate, w_attn_out, conv_w,
           w_conv_out, w_mix_out, norm_ffn_pre, norm_ffn_post, w_ffn_up, w_ffn_down):
    batch, seq, d = x.shape
    depth = w_in.shape[0]
    assert seq % (SSM_CHUNK * SSM_ROWS) == 0 and seq % TOK_TILE == 0 and seq % ATT_BLOCK == 0
    t = batch * seq
    row = lambda a: a.reshape(1, -1).astype(F32)
    x2d = x.reshape(t, d)
    for l in range(depth):
        u8, q, k, v, cp = _proj(x2d, row(norm_mix_pre[l]), w_in[l].astype(BF16), conv_w[l].astype(F32),
                                seq=seq)

        mats = _s5_matrices(ssm_a_re[l], ssm_a_im[l], ssm_log_dt[l], ssm_b_re[l], ssm_b_im[l],
                            ssm_c_re[l], ssm_c_im[l], ssm_d[l])
        ys = _s5(u8, *mats, batch=batch)

        oa = _attn(q, k, v, batch=batch, seq=seq)

        x2d = _merge(x2d, ys, oa, cp, row(norm_mix_pre[l]), w_gate[l].astype(BF16), row(b_gate[l]),
                     w_glu_val[l].astype(BF16), w_glu_gate[l].astype(BF16), w_attn_out[l].astype(BF16),
                     w_conv_out[l].astype(BF16), w_mix_out[l].astype(BF16), row(norm_mix_post[l]))
        x2d = _ffn(x2d, row(norm_ffn_pre[l]), w_ffn_up[l].astype(BF16), w_ffn_down[l].astype(BF16),
                   row(norm_ffn_post[l]))
    return x2d.reshape(batch, seq, d)
```

```python
import functools

import jax
import jax.numpy as jnp
from jax import lax
from jax.experimental import pallas as pl
from jax.experimental.pallas import tpu as pltpu

F32 = jnp.float32
BF16 = jnp.bfloat16

EPS = 1e-6
SSM_GROUP = 16
SSM_GROUPS = 16
D_SSM = SSM_GROUP * SSM_GROUPS
SSM_STATE = 64
SB_HEADS = 8
SB_HEAD_DIM = 64
D_SB = SB_HEADS * SB_HEAD_DIM
D_CONV = 256

V7X_LANES = 128
V7X_VMEM_LIMIT = 56 * 1024 * 1024

SSM_CHUNK = 8
SSM_ROWS = 256
TOK_TILE = 512
ATT_BLOCK = 256
HEADS_PER_STEP = V7X_LANES // SB_HEAD_DIM
ATT_GROUPS = 2
F32_EXP_UNDERFLOW = 104.0
NO_TILE_BIAS = 1e30


def _rms(x, g):
    return x * lax.rsqrt(jnp.mean(x * x, axis=-1, keepdims=True) + EPS) * g


def _sigmoid(x):
    return 1.0 / (1.0 + jnp.exp(-x))


def _const_spec(shape):
    return pl.BlockSpec(shape, lambda *_: (0,) * len(shape), pipeline_mode=pl.Buffered(1))


def _proj_kernel(x_ref, g_ref, win_ref, cw_ref, u8_ref, q_ref, k_ref, v_ref, cp_ref, z_scr, u_scr,
                 *, tiles_per_seq):
    tm = x_ref.shape[0]
    h = _rms(x_ref[...], g_ref[...]).astype(BF16)
    p = jnp.dot(h, win_ref[...], preferred_element_type=F32)
    o = 0
    for hf in range(D_SSM // V7X_LANES):
        u_scr[hf] = p[:, hf * V7X_LANES:(hf + 1) * V7X_LANES]
    for pos in range(SSM_CHUNK):
        for hf in range(D_SSM // V7X_LANES):
            lo = pos * D_SSM + hf * V7X_LANES
            u8_ref[:, lo:lo + V7X_LANES] = u_scr[
                hf, pl.ds(pos, tm // SSM_CHUNK, stride=SSM_CHUNK), :].astype(BF16)
    o += D_SSM
    q_ref[...] = (p[:, o:o + D_SB] * (SB_HEAD_DIM ** -0.5)).astype(BF16); o += D_SB
    k_ref[...] = p[:, o:o + D_SB].astype(BF16); o += D_SB
    v_ref[...] = p[:, o:o + D_SB].astype(BF16); o += D_SB
    cb = p[:, o:o + D_CONV]; o += D_CONV
    cc = p[:, o:o + D_CONV]; o += D_CONV
    cx = p[:, o:o + D_CONV]

    @pl.when(pl.program_id(0) % tiles_per_seq == 0)
    def _():
        z_scr[0:8, :] = jnp.zeros((8, D_CONV), F32)

    z = cc * cx
    z_scr[8:8 + tm, :] = z
    z1 = z_scr[7:7 + tm, :]
    z2 = z_scr[6:6 + tm, :]
    cw = cw_ref[...]
    y = cw[0:1, :] * z2 + cw[1:2, :] * z1 + cw[2:3, :] * z
    cp_ref[...] = (cb * y).astype(BF16)
    z_scr[0:8, :] = z_scr[tm:tm + 8, :]


def _proj(x2d, g, win, cw, *, seq):
    t, d = x2d.shape
    tm = TOK_TILE
    tok = lambda w: pl.BlockSpec((tm, w), lambda i: (i, 0))
    widths = [D_SB, D_SB, D_SB, D_CONV]
    lc = SSM_CHUNK
    return pl.pallas_call(
        functools.partial(_proj_kernel, tiles_per_seq=seq // tm),
        grid=(t // tm,),
        in_specs=[tok(d), _const_spec((1, d)), _const_spec(win.shape), _const_spec(cw.shape)],
        out_specs=[pl.BlockSpec((tm // lc, lc * D_SSM), lambda i: (i, 0))] + [tok(w) for w in widths],
        out_shape=[jax.ShapeDtypeStruct((t // lc, lc * D_SSM), BF16)]
        + [jax.ShapeDtypeStruct((t, w), BF16) for w in widths],
        scratch_shapes=[pltpu.VMEM((tm + 8, D_CONV), F32),
                        pltpu.VMEM((D_SSM // V7X_LANES, tm, V7X_LANES), F32)],
        compiler_params=pltpu.CompilerParams(dimension_semantics=("arbitrary",),
                                             vmem_limit_bytes=V7X_VMEM_LIMIT),
        name="proj",
    )(x2d, g, win, cw)


def _s5_kernel(u_ref, bexp_ref, cexp_ref, kt_ref, a_ref, d_ref, y_ref, e_scr, st_scr, y_scr):
    rows = u_ref.shape[0]
    half = st_scr.shape[1] // 2

    @pl.when(pl.program_id(1) == 0)
    def _():
        st_scr[...] = jnp.zeros(st_scr.shape, F32)

    u = u_ref[...]
    e_scr[...] = jnp.dot(u, bexp_ref[...], preferred_element_type=F32)
    a_re = a_ref[0:1, 0:half]
    a_im = a_ref[0:1, half:]

    def step(j, carry):
        s_re, s_im = carry
        e = e_scr[pl.ds(j, 1), :]
        e_scr[pl.ds(j, 1), :] = jnp.concatenate([s_re, s_im], axis=1)
        n_re = a_re * s_re - a_im * s_im + e[:, 0:half]
        n_im = a_re * s_im + a_im * s_re + e[:, half:]
        return n_re, n_im

    s_re, s_im = lax.fori_loop(0, rows, step, (st_scr[0:1, 0:half], st_scr[0:1, half:]))
    st_scr[0:1, 0:half] = s_re
    st_scr[0:1, half:] = s_im

    y = jnp.dot(e_scr[...].astype(BF16), cexp_ref[...], preferred_element_type=F32)
    y = y + jnp.dot(u, kt_ref[...], preferred_element_type=F32)
    y = y + d_ref[...] * u.astype(F32)
    y = jax.nn.gelu(y)
    for pos in range(SSM_CHUNK):
        for hf in range(D_SSM // V7X_LANES):
            lo = pos * D_SSM + hf * V7X_LANES
            y_scr[hf, pl.ds(pos, rows, stride=SSM_CHUNK), :] = y[:, lo:lo + V7X_LANES]
    for hf in range(D_SSM // V7X_LANES):
        y_ref[:, hf * V7X_LANES:(hf + 1) * V7X_LANES] = y_scr[hf].astype(BF16)


def _s5(u8, bexp, cexp, ktoep, a8, d8, *, batch):
    r, w = u8.shape
    rows = SSM_ROWS
    steps = r // batch // rows
    lc = SSM_CHUNK
    return pl.pallas_call(
        _s5_kernel,
        grid=(batch, steps),
        in_specs=[pl.BlockSpec((rows, w), lambda b, i: (b * steps + i, 0)),
                  _const_spec(bexp.shape), _const_spec(cexp.shape), _const_spec(ktoep.shape),
                  _const_spec(a8.shape), _const_spec(d8.shape)],
        out_specs=pl.BlockSpec((rows * lc, D_SSM), lambda b, i: (b * steps + i, 0)),
        out_shape=jax.ShapeDtypeStruct((r * lc, D_SSM), BF16),
        scratch_shapes=[pltpu.VMEM((rows, bexp.shape[1]), F32), pltpu.VMEM((8, bexp.shape[1]), F32),
                        pltpu.VMEM((D_SSM // V7X_LANES, rows * lc, V7X_LANES), F32)],
        compiler_params=pltpu.CompilerParams(dimension_semantics=("arbitrary", "arbitrary"),
                                             vmem_limit_bytes=V7X_VMEM_LIMIT),
        name="s5",
    )(u8, bexp, cexp, ktoep, a8, d8)


def _s5_matrices(a_re, a_im, log_dt, b_re, b_im, c_re, c_im, d_skip):
    hp = lax.Precision.HIGHEST
    lc = SSM_CHUNK
    dt = jnp.exp(log_dt)[:, None]
    mag = jnp.exp(a_re * dt)
    ab_re = mag * jnp.cos(a_im * dt)
    ab_im = mag * jnp.sin(a_im * dt)
    den = a_re * a_re + a_im * a_im
    xr = ab_re - 1.0
    coef_re = (xr * a_re + ab_im * a_im) / den
    coef_im = (ab_im * a_re - xr * a_im) / den
    bb_re = coef_re[..., None] * b_re - coef_im[..., None] * b_im
    bb_im = coef_re[..., None] * b_im + coef_im[..., None] * b_re
    p_re = [jnp.ones_like(ab_re)]
    p_im = [jnp.zeros_like(ab_im)]
    for _ in range(lc):
        r, i = p_re[-1], p_im[-1]
        p_re.append(r * ab_re - i * ab_im)
        p_im.append(r * ab_im + i * ab_re)
    p_re = jnp.stack(p_re)
    p_im = jnp.stack(p_im)
    gn = SSM_GROUPS * SSM_STATE
    lw = lc * D_SSM

    def expand(compact, row_w, col_w, cols):
        n_in = compact.shape[1]
        src = lax.broadcasted_iota(jnp.int32, (n_in, cols), 0)
        dst = lax.broadcasted_iota(jnp.int32, (n_in, cols), 1)
        per = cols // SSM_GROUPS // col_w
        sel = (src == (dst // (SSM_GROUPS * col_w)) * col_w + dst % col_w).astype(F32)
        assert per * col_w == n_in
        full = jnp.dot(compact, sel, precision=hp)
        rg = lax.broadcasted_iota(jnp.int32, full.shape, 0) // row_w % SSM_GROUPS
        cg = lax.broadcasted_iota(jnp.int32, full.shape, 1) // col_w % SSM_GROUPS
        return jnp.where(rg == cg, full, 0.0).astype(BF16)

    pr = p_re[lc - 1::-1][:lc]
    pi = p_im[lc - 1::-1][:lc]
    t_re = pr[:, :, :, None] * bb_re[None] - pi[:, :, :, None] * bb_im[None]
    t_im = pr[:, :, :, None] * bb_im[None] + pi[:, :, :, None] * bb_re[None]
    t1 = jnp.stack([t_re, t_im], axis=0)
    bexp = expand(t1.transpose(1, 2, 4, 0, 3).reshape(lw, 2 * SSM_STATE), SSM_GROUP, SSM_STATE, 2 * gn)

    qr = p_re[1:lc + 1]
    qi = p_im[1:lc + 1]
    ca_re = c_re[None] * qr[:, :, None, :] - c_im[None] * qi[:, :, None, :]
    ca_im = c_re[None] * qi[:, :, None, :] + c_im[None] * qr[:, :, None, :]
    t2 = jnp.stack([ca_re, -ca_im], axis=0)
    cexp = expand(t2.transpose(0, 2, 4, 1, 3).reshape(2 * gn, lc * SSM_GROUP), SSM_STATE, SSM_GROUP, lw)

    cp_re = c_re[None] * p_re[:lc, :, None, :] - c_im[None] * p_im[:lc, :, None, :]
    cp_im = c_re[None] * p_im[:lc, :, None, :] + c_im[None] * p_re[:lc, :, None, :]
    ktap = (jnp.einsum('tgcn,gnd->tgcd', cp_re, bb_re, precision=hp)
            - jnp.einsum('tgcn,gnd->tgcd', cp_im, bb_im, precision=hp))
    pos = jnp.arange(lc)
    tau = pos[None, :] - pos[:, None]
    taps = jnp.where((tau >= 0)[:, :, None, None, None], ktap[jnp.clip(tau, 0, lc - 1)], 0.0)
    ktoep = expand(taps.transpose(0, 2, 4, 1, 3).reshape(lw, lc * SSM_GROUP), SSM_GROUP, SSM_GROUP, lw)

    a8 = jnp.concatenate([p_re[lc].reshape(1, gn), p_im[lc].reshape(1, gn)], axis=1)
    d8 = jnp.tile(d_skip.reshape(1, D_SSM), (1, lc))
    return bexp.astype(BF16), cexp.astype(BF16), ktoep.astype(BF16), a8, d8


def _softplus(z):
    return jnp.maximum(z, 0.0) + jnp.log(1.0 + jnp.exp(-jnp.abs(z)))


def _attn_kernel(q_ref, k_ref, v_ref, o_ref):
    bq = q_ref.shape[0]
    lanes = V7X_LANES
    bk = bq
    i = pl.program_id(2)
    lane = lax.broadcasted_iota(jnp.int32, (bq, lanes), 1)
    causal = (lax.broadcasted_iota(jnp.int32, (bq, bk), 1)
              < lax.broadcasted_iota(jnp.int32, (bq, bk), 0))
    suffix = (lax.broadcasted_iota(jnp.int32, (bk, bk), 0)
              > lax.broadcasted_iota(jnp.int32, (bk, bk), 1)).astype(BF16)
    heads = range(ATT_GROUPS * HEADS_PER_STEP)
    cols = [slice(hh // HEADS_PER_STEP * lanes, (hh // HEADS_PER_STEP + 1) * lanes) for hh in heads]
    qm = [jnp.where(lane // SB_HEAD_DIM == hh % HEADS_PER_STEP, q_ref[:, cols[hh]], 0).astype(BF16)
          for hh in heads]

    def tile(hh, kb, run, acc, masked):
        rows = pl.ds(pl.multiple_of(kb * bk, bk), bk)
        z = lax.dot_general(qm[hh], k_ref[rows, cols[hh]], (((1,), (1,)), ((), ())),
                            preferred_element_type=F32)
        sp = _softplus(z)
        if masked:
            sp = jnp.where(causal, sp, 0.0)
        c = jnp.dot(sp.astype(BF16), suffix, preferred_element_type=F32)
        w = jnp.exp(z - sp - c - jnp.concatenate([run] * (bk // lanes), axis=1))
        if masked:
            w = jnp.where(causal, w, 0.0)
        acc = acc + jnp.dot(w.astype(BF16), v_ref[rows, cols[hh]], preferred_element_type=F32)
        return run + jnp.broadcast_to(c[:, 0:1] + sp[:, 0:1], run.shape), acc

    run = [jnp.zeros((bq, lanes), F32) for _ in heads]
    acc = [jnp.zeros((bq, lanes), F32) for _ in heads]
    for hh in heads:
        run[hh], acc[hh] = tile(hh, i, run[hh], acc[hh], True)
    left = jnp.maximum(i - 1, 0)
    bias = jnp.where(i > 0, 0.0, NO_TILE_BIAS).astype(F32)
    for hh in heads:
        run[hh], acc[hh] = tile(hh, left, run[hh] + bias, acc[hh], False)

    def more(st):
        j, runs, _ = st
        low = functools.reduce(jnp.minimum, [jnp.min(r) for r in runs])
        return jnp.logical_and(j <= i, low < F32_EXP_UNDERFLOW)

    def step(st):
        j, runs, accs = st
        out = [tile(hh, i - j, runs[hh], accs[hh], False) for hh in heads]
        return j + 1, tuple(o[0] for o in out), tuple(o[1] for o in out)

    _, _, acc = lax.while_loop(more, step, (jnp.int32(2), tuple(run), tuple(acc)))
    for g in range(ATT_GROUPS):
        o_ref[:, g * lanes:(g + 1) * lanes] = jnp.where(
            lane < SB_HEAD_DIM, acc[g * HEADS_PER_STEP], acc[g * HEADS_PER_STEP + 1]).astype(BF16)


def _attn(q, k, v, *, batch, seq):
    bq = ATT_BLOCK
    nq = seq // bq
    assert HEADS_PER_STEP == 2
    wcols = ATT_GROUPS * V7X_LANES
    return pl.pallas_call(
        _attn_kernel,
        grid=(batch, D_SB // wcols, nq),
        in_specs=[pl.BlockSpec((bq, wcols), lambda b, p, i: (b * nq + i, p)),
                  pl.BlockSpec((seq, wcols), lambda b, p, i: (b, p)),
                  pl.BlockSpec((seq, wcols), lambda b, p, i: (b, p))],
        out_specs=pl.BlockSpec((bq, wcols), lambda b, p, i: (b * nq + i, p)),
        out_shape=jax.ShapeDtypeStruct((batch * seq, D_SB), BF16),
        compiler_params=pltpu.CompilerParams(
            dimension_semantics=("arbitrary", "arbitrary", "arbitrary"),
            vmem_limit_bytes=V7X_VMEM_LIMIT),
        name="attn",
    )(q, k, v)


def _merge_kernel(x_ref, ys_ref, oa_ref, cp_ref, gpre_ref, wg_ref, bg_ref, wval_ref, wglu_ref,
                  wao_ref, wco_ref, wmo_ref, gpost_ref, o_ref):
    d = x_ref.shape[1]
    x = x_ref[...]
    h = _rms(x, gpre_ref[...]).astype(BF16)
    gates = _sigmoid(jnp.dot(h, wg_ref[...], preferred_element_type=F32) + bg_ref[...])
    ys = ys_ref[...]
    y_ssm = (jnp.dot(ys, wval_ref[...], preferred_element_type=F32)
             * _sigmoid(jnp.dot(ys, wglu_ref[...], preferred_element_type=F32)))
    y_sb = jnp.dot(oa_ref[...], wao_ref[...], preferred_element_type=F32)
    y_conv = jnp.dot(cp_ref[...], wco_ref[...], preferred_element_type=F32)
    merged = gates[:, 0:d] * y_ssm + gates[:, d:2 * d] * y_sb + gates[:, 2 * d:] * y_conv
    m = jnp.dot(merged.astype(BF16), wmo_ref[...], preferred_element_type=F32)
    o_ref[...] = x + _rms(m, gpost_ref[...])


def _merge(x2d, ys, oa, cp, gpre, wg, bg, wval, wglu, wao, wco, wmo, gpost):
    t, d = x2d.shape
    tm = TOK_TILE
    tok = lambda w: pl.BlockSpec((tm, w), lambda i: (i, 0))
    consts = [gpre, wg, bg, wval, wglu, wao, wco, wmo, gpost]
    return pl.pallas_call(
        _merge_kernel,
        grid=(t // tm,),
        in_specs=[tok(d), tok(D_SSM), tok(D_SB), tok(D_CONV)] + [_const_spec(c.shape) for c in consts],
        out_specs=tok(d),
        out_shape=jax.ShapeDtypeStruct((t, d), F32),
        compiler_params=pltpu.CompilerParams(dimension_semantics=("arbitrary",),
                                             vmem_limit_bytes=V7X_VMEM_LIMIT),
        name="merge",
    )(x2d, ys, oa, cp, *consts)


def _ffn_kernel(x_ref, gpre_ref, wup_ref, wdn_ref, gpost_ref, o_ref):
    x = x_ref[...]
    h = _rms(x, gpre_ref[...]).astype(BF16)
    a = jnp.maximum(jnp.dot(h, wup_ref[...], preferred_element_type=F32), 0.0)
    f = jnp.dot((a * a).astype(BF16), wdn_ref[...], preferred_element_type=F32)
    o_ref[...] = x + _rms(f, gpost_ref[...])


def _ffn(x2d, gpre, wup, wdn, gpost):
    t, d = x2d.shape
    tm = TOK_TILE
    tok = pl.BlockSpec((tm, d), lambda i: (i, 0))
    consts = [gpre, wup, wdn, gpost]
    return pl.pallas_call(
        _ffn_kernel,
        grid=(t // tm,),
        in_specs=[tok] + [_const_spec(c.shape) for c in consts],
        out_specs=tok,
        out_shape=jax.ShapeDtypeStruct((t, d), F32),
        compiler_params=pltpu.CompilerParams(dimension_semantics=("arbitrary",),
                                             vmem_limit_bytes=V7X_VMEM_LIMIT),
        name="ffn",
    )(x2d, *consts)


def kernel(x, norm_mix_pre, norm_mix_post, w_in, w_gate, b_gate, ssm_a_re, ssm_a_im, ssm_log_dt,
           ssm_b_re, ssm_b_im, ssm_c_re, ssm_c_im, ssm_d, w_glu_val, w_glu_gate, w_attn_out, conv_w,
           w_conv_out, w_mix_out, norm_ffn_pre, norm_ffn_post, w_ffn_up, w_ffn_down):
    batch, seq, d = x.shape
    depth = w_in.shape[0]
    assert seq % (SSM_CHUNK * SSM_ROWS) == 0 and seq % TOK_TILE == 0 and seq % ATT_BLOCK == 0
    t = batch * seq
    row = lambda a: a.reshape(1, -1).astype(F32)
    x2d = x.reshape(t, d)
    for l in range(depth):
        u8, q, k, v, cp = _proj(x2d, row(norm_mix_pre[l]), w_in[l].astype(BF16), conv_w[l].astype(F32),
                                seq=seq)

        mats = _s5_matrices(ssm_a_re[l], ssm_a_im[l], ssm_log_dt[l], ssm_b_re[l], ssm_b_im[l],
                            ssm_c_re[l], ssm_c_im[l], ssm_d[l])
        ys = _s5(u8, *mats, batch=batch)

        oa = _attn(q, k, v, batch=batch, seq=seq)

        x2d = _merge(x2d, ys, oa, cp, row(norm_mix_pre[l]), w_gate[l].astype(BF16), row(b_gate[l]),
                     w_glu_val[l].astype(BF16), w_glu_gate[l].astype(BF16), w_attn_out[l].astype(BF16),
                     w_conv_out[l].astype(BF16), w_mix_out[l].astype(BF16), row(norm_mix_post[l]))
        x2d = _ffn(x2d, row(norm_ffn_pre[l]), w_ffn_up[l].astype(BF16), w_ffn_down[l].astype(BF16),
                   row(norm_ffn_post[l]))
    return x2d.reshape(batch, seq, d)
```

```python
import functools

import jax
import jax.numpy as jnp
from jax import lax
from jax.experimental import pallas as pl
from jax.experimental.pallas import tpu as pltpu

F32 = jnp.float32
BF16 = jnp.bfloat16

EPS = 1e-6
SSM_GROUP = 16
SSM_GROUPS = 16
D_SSM = SSM_GROUP * SSM_GROUPS
SSM_STATE = 64
SB_HEADS = 8
SB_HEAD_DIM = 64
D_SB = SB_HEADS * SB_HEAD_DIM
D_CONV = 256

V7X_LANES = 128
V7X_VMEM_LIMIT = 56 * 1024 * 1024

SSM_CHUNK = 8
SSM_ROWS = 256
TOK_TILE = 512
ATT_BLOCK = 256
HEADS_PER_STEP = V7X_LANES // SB_HEAD_DIM
ATT_GROUPS = 2
F32_EXP_UNDERFLOW = 104.0
NO_TILE_BIAS = 1e30


def _rms(x, g):
    return x * lax.rsqrt(jnp.mean(x * x, axis=-1, keepdims=True) + EPS) * g


def _sigmoid(x):
    return 1.0 / (1.0 + jnp.exp(-x))


def _const_spec(shape):
    return pl.BlockSpec(shape, lambda *_: (0,) * len(shape), pipeline_mode=pl.Buffered(1))


def _proj_kernel(x_ref, g_ref, win_ref, cw_ref, u8_ref, q_ref, k_ref, v_ref, cp_ref, z_scr, u_scr,
                 *, tiles_per_seq):
    tm = x_ref.shape[0]
    h = _rms(x_ref[...], g_ref[...]).astype(BF16)
    p = jnp.dot(h, win_ref[...], preferred_element_type=F32)
    o = 0
    for hf in range(D_SSM // V7X_LANES):
        u_scr[hf] = p[:, hf * V7X_LANES:(hf + 1) * V7X_LANES]
    for pos in range(SSM_CHUNK):
        for hf in range(D_SSM // V7X_LANES):
            lo = pos * D_SSM + hf * V7X_LANES
            u8_ref[:, lo:lo + V7X_LANES] = u_scr[
                hf, pl.ds(pos, tm // SSM_CHUNK, stride=SSM_CHUNK), :].astype(BF16)
    o += D_SSM
    q_ref[...] = (p[:, o:o + D_SB] * (SB_HEAD_DIM ** -0.5)).astype(BF16); o += D_SB
    k_ref[...] = p[:, o:o + D_SB].astype(BF16); o += D_SB
    v_ref[...] = p[:, o:o + D_SB].astype(BF16); o += D_SB
    cb = p[:, o:o + D_CONV]; o += D_CONV
    cc = p[:, o:o + D_CONV]; o += D_CONV
    cx = p[:, o:o + D_CONV]

    @pl.when(pl.program_id(0) % tiles_per_seq == 0)
    def _():
        z_scr[0:8, :] = jnp.zeros((8, D_CONV), F32)

    z = cc * cx
    z_scr[8:8 + tm, :] = z
    z1 = z_scr[7:7 + tm, :]
    z2 = z_scr[6:6 + tm, :]
    cw = cw_ref[...]
    y = cw[0:1, :] * z2 + cw[1:2, :] * z1 + cw[2:3, :] * z
    cp_ref[...] = (cb * y).astype(BF16)
    z_scr[0:8, :] = z_scr[tm:tm + 8, :]


def _proj(x2d, g, win, cw, *, seq):
    t, d = x2d.shape
    tm = TOK_TILE
    tok = lambda w: pl.BlockSpec((tm, w), lambda i: (i, 0))
    widths = [D_SB, D_SB, D_SB, D_CONV]
    lc = SSM_CHUNK
    return pl.pallas_call(
        functools.partial(_proj_kernel, tiles_per_seq=seq // tm),
        grid=(t // tm,),
        in_specs=[tok(d), _const_spec((1, d)), _const_spec(win.shape), _const_spec(cw.shape)],
        out_specs=[pl.BlockSpec((tm // lc, lc * D_SSM), lambda i: (i, 0))] + [tok(w) for w in widths],
        out_shape=[jax.ShapeDtypeStruct((t // lc, lc * D_SSM), BF16)]
        + [jax.ShapeDtypeStruct((t, w), BF16) for w in widths],
        scratch_shapes=[pltpu.VMEM((tm + 8, D_CONV), F32),
                        pltpu.VMEM((D_SSM // V7X_LANES, tm, V7X_LANES), F32)],
        compiler_params=pltpu.CompilerParams(dimension_semantics=("arbitrary",),
                                             vmem_limit_bytes=V7X_VMEM_LIMIT),
        name="proj",
    )(x2d, g, win, cw)


def _s5_kernel(u_ref, bexp_ref, cexp_ref, kt_ref, a_ref, d_ref, y_ref, e_scr, st_scr, y_scr):
    rows = u_ref.shape[0]
    half = st_scr.shape[1] // 2

    @pl.when(pl.program_id(1) == 0)
    def _():
        st_scr[...] = jnp.zeros(st_scr.shape, F32)

    u = u_ref[...]
    e_scr[...] = jnp.dot(u, bexp_ref[...], preferred_element_type=F32)
    a_re = a_ref[0:1, 0:half]
    a_im = a_ref[0:1, half:]

    def step(j, carry):
        s_re, s_im = carry
        e = e_scr[pl.ds(j, 1), :]
        e_scr[pl.ds(j, 1), :] = jnp.concatenate([s_re, s_im], axis=1)
        n_re = a_re * s_re - a_im * s_im + e[:, 0:half]
        n_im = a_re * s_im + a_im * s_re + e[:, half:]
        return n_re, n_im

    s_re, s_im = lax.fori_loop(0, rows, step, (st_scr[0:1, 0:half], st_scr[0:1, half:]))
    st_scr[0:1, 0:half] = s_re
    st_scr[0:1, half:] = s_im

    y = jnp.dot(e_scr[...].astype(BF16), cexp_ref[...], preferred_element_type=F32)
    y = y + jnp.dot(u, kt_ref[...], preferred_element_type=F32)
    y = y + d_ref[...] * u.astype(F32)
    y = jax.nn.gelu(y)
    for pos in range(SSM_CHUNK):
        for hf in range(D_SSM // V7X_LANES):
            lo = pos * D_SSM + hf * V7X_LANES
            y_scr[hf, pl.ds(pos, rows, stride=SSM_CHUNK), :] = y[:, lo:lo + V7X_LANES]
    for hf in range(D_SSM // V7X_LANES):
        y_ref[:, hf * V7X_LANES:(hf + 1) * V7X_LANES] = y_scr[hf].astype(BF16)


def _s5(u8, bexp, cexp, ktoep, a8, d8, *, batch):
    r, w = u8.shape
    rows = SSM_ROWS
    steps = r // batch // rows
    lc = SSM_CHUNK
    return pl.pallas_call(
        _s5_kernel,
        grid=(batch, steps),
        in_specs=[pl.BlockSpec((rows, w), lambda b, i: (b * steps + i, 0)),
                  _const_spec(bexp.shape), _const_spec(cexp.shape), _const_spec(ktoep.shape),
                  _const_spec(a8.shape), _const_spec(d8.shape)],
        out_specs=pl.BlockSpec((rows * lc, D_SSM), lambda b, i: (b * steps + i, 0)),
        out_shape=jax.ShapeDtypeStruct((r * lc, D_SSM), BF16),
        scratch_shapes=[pltpu.VMEM((rows, bexp.shape[1]), F32), pltpu.VMEM((8, bexp.shape[1]), F32),
                        pltpu.VMEM((D_SSM // V7X_LANES, rows * lc, V7X_LANES), F32)],
        compiler_params=pltpu.CompilerParams(dimension_semantics=("arbitrary", "arbitrary"),
                                             vmem_limit_bytes=V7X_VMEM_LIMIT),
        name="s5",
    )(u8, bexp, cexp, ktoep, a8, d8)


def _s5_matrices(a_re, a_im, log_dt, b_re, b_im, c_re, c_im, d_skip):
    hp = lax.Precision.HIGHEST
    lc = SSM_CHUNK
    dt = jnp.exp(log_dt)[:, None]
    mag = jnp.exp(a_re * dt)
    ab_re = mag * jnp.cos(a_im * dt)
    ab_im = mag * jnp.sin(a_im * dt)
    den = a_re * a_re + a_im * a_im
    xr = ab_re - 1.0
    coef_re = (xr * a_re + ab_im * a_im) / den
    coef_im = (ab_im * a_re - xr * a_im) / den
    bb_re = coef_re[..., None] * b_re - coef_im[..., None] * b_im
    bb_im = coef_re[..., None] * b_im + coef_im[..., None] * b_re
    p_re = [jnp.ones_like(ab_re)]
    p_im = [jnp.zeros_like(ab_im)]
    for _ in range(lc):
        r, i = p_re[-1], p_im[-1]
        p_re.append(r * ab_re - i * ab_im)
        p_im.append(r * ab_im + i * ab_re)
    p_re = jnp.stack(p_re)
    p_im = jnp.stack(p_im)
    gn = SSM_GROUPS * SSM_STATE
    lw = lc * D_SSM

    def expand(compact, row_w, col_w, cols):
        n_in = compact.shape[1]
        src = lax.broadcasted_iota(jnp.int32, (n_in, cols), 0)
        dst = lax.broadcasted_iota(jnp.int32, (n_in, cols), 1)
        per = cols // SSM_GROUPS // col_w
        sel = (src == (dst // (SSM_GROUPS * col_w)) * col_w + dst % col_w).astype(F32)
        assert per * col_w == n_in
        full = jnp.dot(compact, sel, precision=hp)
        rg = lax.broadcasted_iota(jnp.int32, full.shape, 0) // row_w % SSM_GROUPS
        cg = lax.broadcasted_iota(jnp.int32, full.shape, 1) // col_w % SSM_GROUPS
        return jnp.where(rg == cg, full, 0.0).astype(BF16)

    pr = p_re[lc - 1::-1][:lc]
    pi = p_im[lc - 1::-1][:lc]
    t_re = pr[:, :, :, None] * bb_re[None] - pi[:, :, :, None] * bb_im[None]
    t_im = pr[:, :, :, None] * bb_im[None] + pi[:, :, :, None] * bb_re[None]
    t1 = jnp.stack([t_re, t_im], axis=0)
    bexp = expand(t1.transpose(1, 2, 4, 0, 3).reshape(lw, 2 * SSM_STATE), SSM_GROUP, SSM_STATE, 2 * gn)

    qr = p_re[1:lc + 1]
    qi = p_im[1:lc + 1]
    ca_re = c_re[None] * qr[:, :, None, :] - c_im[None] * qi[:, :, None, :]
    ca_im = c_re[None] * qi[:, :, None, :] + c_im[None] * qr[:, :, None, :]
    t2 = jnp.stack([ca_re, -ca_im], axis=0)
    cexp = expand(t2.transpose(0, 2, 4, 1, 3).reshape(2 * gn, lc * SSM_GROUP), SSM_STATE, SSM_GROUP, lw)

    cp_re = c_re[None] * p_re[:lc, :, None, :] - c_im[None] * p_im[:lc, :, None, :]
    cp_im = c_re[None] * p_im[:lc, :, None, :] + c_im[None] * p_re[:lc, :, None, :]
    ktap = (jnp.einsum('tgcn,gnd->tgcd', cp_re, bb_re, precision=hp)
            - jnp.einsum('tgcn,gnd->tgcd', cp_im, bb_im, precision=hp))
    pos = jnp.arange(lc)
    tau = pos[None, :] - pos[:, None]
    taps = jnp.where((tau >= 0)[:, :, None, None, None], ktap[jnp.clip(tau, 0, lc - 1)], 0.0)
    ktoep = expand(taps.transpose(0, 2, 4, 1, 3).reshape(lw, lc * SSM_GROUP), SSM_GROUP, SSM_GROUP, lw)

    a8 = jnp.concatenate([p_re[lc].reshape(1, gn), p_im[lc].reshape(1, gn)], axis=1)
    d8 = jnp.tile(d_skip.reshape(1, D_SSM), (1, lc))
    return bexp.astype(BF16), cexp.astype(BF16), ktoep.astype(BF16), a8, d8


def _softplus(z):
    return jnp.maximum(z, 0.0) + jnp.log(1.0 + jnp.exp(-jnp.abs(z)))


def _attn_kernel(q_ref, k_ref, v_ref, o_ref):
    bq = q_ref.shape[0]
    lanes = V7X_LANES
    bk = bq
    i = pl.program_id(2)
    groups = range(ATT_GROUPS)
    lane = lax.broadcasted_iota(jnp.int32, (bq, lanes), 1)
    causal = (lax.broadcasted_iota(jnp.int32, (bq, bk), 1)
              < lax.broadcasted_iota(jnp.int32, (bq, bk), 0))
    causal = jnp.concatenate([causal] * HEADS_PER_STEP, axis=0)
    suffix = (lax.broadcasted_iota(jnp.int32, (bk, bk), 0)
              > lax.broadcasted_iota(jnp.int32, (bk, bk), 1)).astype(BF16)
    cols = [slice(g * lanes, (g + 1) * lanes) for g in groups]
    qs = [jnp.concatenate([jnp.where(lane // SB_HEAD_DIM == hh, q_ref[:, cols[g]], 0).astype(BF16)
                           for hh in range(HEADS_PER_STEP)], axis=0) for g in groups]

    def scores(g, kb, masked):
        rows = pl.ds(pl.multiple_of(kb * bk, bk), bk)
        z = lax.dot_general(qs[g], k_ref[rows, cols[g]], (((1,), (1,)), ((), ())),
                            preferred_element_type=F32)
        sp = _softplus(z)
        if masked:
            sp = jnp.where(causal, sp, 0.0)
        return z, sp

    def finish(g, kb, z, sp, c, run, acc, masked):
        rows = pl.ds(pl.multiple_of(kb * bk, bk), bk)
        w = jnp.exp(z - sp - c - jnp.concatenate([run] * (bk // lanes), axis=1))
        if masked:
            w = jnp.where(causal, w, 0.0)
        acc = acc + jnp.dot(w.astype(BF16), v_ref[rows, cols[g]], preferred_element_type=F32)
        return run + jnp.broadcast_to(c[:, 0:1] + sp[:, 0:1], run.shape), acc

    left = jnp.maximum(i - 1, 0)
    bias = jnp.where(i > 0, 0.0, NO_TILE_BIAS).astype(F32)
    first = [(g, kb, masked) for kb, masked in ((i, True), (left, False)) for g in groups]
    zs = [scores(*t) for t in first]
    c_all = jnp.dot(jnp.concatenate([sp.astype(BF16) for _, sp in zs], axis=0), suffix,
                    preferred_element_type=F32)
    rows2 = HEADS_PER_STEP * bq
    run = [jnp.zeros((rows2, lanes), F32) for _ in groups]
    acc = [jnp.zeros((rows2, lanes), F32) for _ in groups]
    for n, (g, kb, masked) in enumerate(first):
        r_in = run[g] if masked else run[g] + bias
        run[g], acc[g] = finish(g, kb, zs[n][0], zs[n][1], c_all[n * rows2:(n + 1) * rows2],
                                r_in, acc[g], masked)

    def more(st):
        j, runs, _ = st
        low = functools.reduce(jnp.minimum, [jnp.min(r) for r in runs])
        return jnp.logical_and(j <= i, low < F32_EXP_UNDERFLOW)

    def step(st):
        j, runs, accs = st
        out = []
        for g in groups:
            z, sp = scores(g, i - j, False)
            c = jnp.dot(sp.astype(BF16), suffix, preferred_element_type=F32)
            out.append(finish(g, i - j, z, sp, c, runs[g], accs[g], False))
        return j + 1, tuple(o[0] for o in out), tuple(o[1] for o in out)

    _, _, acc = lax.while_loop(more, step, (jnp.int32(2), tuple(run), tuple(acc)))
    for g in groups:
        o_ref[:, cols[g]] = jnp.where(lane < SB_HEAD_DIM, acc[g][0:bq], acc[g][bq:]).astype(BF16)


def _attn(q, k, v, *, batch, seq):
    bq = ATT_BLOCK
    nq = seq // bq
    assert HEADS_PER_STEP == 2
    wcols = ATT_GROUPS * V7X_LANES
    return pl.pallas_call(
        _attn_kernel,
        grid=(batch, D_SB // wcols, nq),
        in_specs=[pl.BlockSpec((bq, wcols), lambda b, p, i: (b * nq + i, p)),
                  pl.BlockSpec((seq, wcols), lambda b, p, i: (b, p)),
                  pl.BlockSpec((seq, wcols), lambda b, p, i: (b, p))],
        out_specs=pl.BlockSpec((bq, wcols), lambda b, p, i: (b * nq + i, p)),
        out_shape=jax.ShapeDtypeStruct((batch * seq, D_SB), BF16),
        compiler_params=pltpu.CompilerParams(
            dimension_semantics=("arbitrary", "arbitrary", "arbitrary"),
            vmem_limit_bytes=V7X_VMEM_LIMIT),
        name="attn",
    )(q, k, v)


def _merge_kernel(x_ref, ys_ref, oa_ref, cp_ref, gpre_ref, wg_ref, bg_ref, wval_ref, wglu_ref,
                  wao_ref, wco_ref, wmo_ref, gpost_ref, o_ref):
    d = x_ref.shape[1]
    x = x_ref[...]
    h = _rms(x, gpre_ref[...]).astype(BF16)
    gates = _sigmoid(jnp.dot(h, wg_ref[...], preferred_element_type=F32) + bg_ref[...])
    ys = ys_ref[...]
    y_ssm = (jnp.dot(ys, wval_ref[...], preferred_element_type=F32)
             * _sigmoid(jnp.dot(ys, wglu_ref[...], preferred_element_type=F32)))
    y_sb = jnp.dot(oa_ref[...], wao_ref[...], preferred_element_type=F32)
    y_conv = jnp.dot(cp_ref[...], wco_ref[...], preferred_element_type=F32)
    merged = gates[:, 0:d] * y_ssm + gates[:, d:2 * d] * y_sb + gates[:, 2 * d:] * y_conv
    m = jnp.dot(merged.astype(BF16), wmo_ref[...], preferred_element_type=F32)
    o_ref[...] = x + _rms(m, gpost_ref[...])


def _merge(x2d, ys, oa, cp, gpre, wg, bg, wval, wglu, wao, wco, wmo, gpost):
    t, d = x2d.shape
    tm = TOK_TILE
    tok = lambda w: pl.BlockSpec((tm, w), lambda i: (i, 0))
    consts = [gpre, wg, bg, wval, wglu, wao, wco, wmo, gpost]
    return pl.pallas_call(
        _merge_kernel,
        grid=(t // tm,),
        in_specs=[tok(d), tok(D_SSM), tok(D_SB), tok(D_CONV)] + [_const_spec(c.shape) for c in consts],
        out_specs=tok(d),
        out_shape=jax.ShapeDtypeStruct((t, d), F32),
        compiler_params=pltpu.CompilerParams(dimension_semantics=("arbitrary",),
                                             vmem_limit_bytes=V7X_VMEM_LIMIT),
        name="merge",
    )(x2d, ys, oa, cp, *consts)


def _ffn_kernel(x_ref, gpre_ref, wup_ref, wdn_ref, gpost_ref, o_ref):
    x = x_ref[...]
    h = _rms(x, gpre_ref[...]).astype(BF16)
    a = jnp.maximum(jnp.dot(h, wup_ref[...], preferred_element_type=F32), 0.0)
    f = jnp.dot((a * a).astype(BF16), wdn_ref[...], preferred_element_type=F32)
    o_ref[...] = x + _rms(f, gpost_ref[...])


def _ffn(x2d, gpre, wup, wdn, gpost):
    t, d = x2d.shape
    tm = TOK_TILE
    tok = pl.BlockSpec((tm, d), lambda i: (i, 0))
    consts = [gpre, wup, wdn, gpost]
    return pl.pallas_call(
        _ffn_kernel,
        grid=(t // tm,),
        in_specs=[tok] + [_const_spec(c.shape) for c in consts],
        out_specs=tok,
        out_shape=jax.ShapeDtypeStruct((t, d), F32),
        compiler_params=pltpu.CompilerParams(dimension_semantics=("arbitrary",),
                                             vmem_limit_bytes=V7X_VMEM_LIMIT),
        name="ffn",
    )(x2d, *consts)


def kernel(x, norm_mix_pre, norm_mix_post, w_in, w_gate, b_gate, ssm_a_re, ssm_a_im, ssm_log_dt,
           ssm_b_re, ssm_b_im, ssm_c_re, ssm_c_im, ssm_d, w_glu_val, w_glu_gate, w_attn_out, conv_w,
           w_conv_out, w_mix_out, norm_ffn_pre, norm_ffn_post, w_ffn_up, w_ffn_down):
    batch, seq, d = x.shape
    depth = w_in.shape[0]
    assert seq % (SSM_CHUNK * SSM_ROWS) == 0 and seq % TOK_TILE == 0 and seq % ATT_BLOCK == 0
    t = batch * seq
    row = lambda a: a.reshape(1, -1).astype(F32)
    x2d = x.reshape(t, d)
    for l in range(depth):
        u8, q, k, v, cp = _proj(x2d, row(norm_mix_pre[l]), w_in[l].astype(BF16), conv_w[l].astype(F32),
                                seq=seq)

        mats = _s5_matrices(ssm_a_re[l], ssm_a_im[l], ssm_log_dt[l], ssm_b_re[l], ssm_b_im[l],
                            ssm_c_re[l], ssm_c_im[l], ssm_d[l])
        ys = _s5(u8, *mats, batch=batch)

        oa = _attn(q, k, v, batch=batch, seq=seq)

        x2d = _merge(x2d, ys, oa, cp, row(norm_mix_pre[l]), w_gate[l].astype(BF16), row(b_gate[l]),
                     w_glu_val[l].astype(BF16), w_glu_gate[l].astype(BF16), w_attn_out[l].astype(BF16),
                     w_conv_out[l].astype(BF16), w_mix_out[l].astype(BF16), row(norm_mix_post[l]))
        x2d = _ffn(x2d, row(norm_ffn_pre[l]), w_ffn_up[l].astype(BF16), w_ffn_down[l].astype(BF16),
                   row(norm_ffn_post[l]))
    return x2d.reshape(batch, seq, d)
```

```python
import functools

import jax
import jax.numpy as jnp
from jax import lax
from jax.experimental import pallas as pl
from jax.experimental.pallas import tpu as pltpu

F32 = jnp.float32
BF16 = jnp.bfloat16

EPS = 1e-6
SSM_GROUP = 16
SSM_GROUPS = 16
D_SSM = SSM_GROUP * SSM_GROUPS
SSM_STATE = 64
SB_HEADS = 8
SB_HEAD_DIM = 64
D_SB = SB_HEADS * SB_HEAD_DIM
D_CONV = 256

V7X_LANES = 128
V7X_VMEM_LIMIT = 56 * 1024 * 1024

SSM_CHUNK = 8
SSM_ROWS = 256
TOK_TILE = 512
BIG_TILE = 1024
ATT_BLOCK = 256
HEADS_PER_STEP = V7X_LANES // SB_HEAD_DIM
ATT_GROUPS = 2
F32_EXP_UNDERFLOW = 104.0
NO_TILE_BIAS = 1e30


def _rms(x, g):
    return x * lax.rsqrt(jnp.mean(x * x, axis=-1, keepdims=True) + EPS) * g


def _sigmoid(x):
    return 1.0 / (1.0 + jnp.exp(-x))


def _const_spec(shape):
    return pl.BlockSpec(shape, lambda *_: (0,) * len(shape), pipeline_mode=pl.Buffered(1))


def _proj_kernel(x_ref, g_ref, win_ref, cw_ref, u8_ref, q_ref, k_ref, v_ref, cp_ref, z_scr, u_scr,
                 *, tiles_per_seq):
    tm = x_ref.shape[0]
    h = _rms(x_ref[...], g_ref[...]).astype(BF16)
    p = jnp.dot(h, win_ref[...], preferred_element_type=F32)
    o = 0
    for hf in range(D_SSM // V7X_LANES):
        u_scr[hf] = p[:, hf * V7X_LANES:(hf + 1) * V7X_LANES]
    for pos in range(SSM_CHUNK):
        for hf in range(D_SSM // V7X_LANES):
            lo = pos * D_SSM + hf * V7X_LANES
            u8_ref[:, lo:lo + V7X_LANES] = u_scr[
                hf, pl.ds(pos, tm // SSM_CHUNK, stride=SSM_CHUNK), :].astype(BF16)
    o += D_SSM
    q_ref[...] = (p[:, o:o + D_SB] * (SB_HEAD_DIM ** -0.5)).astype(BF16); o += D_SB
    k_ref[...] = p[:, o:o + D_SB].astype(BF16); o += D_SB
    v_ref[...] = p[:, o:o + D_SB].astype(BF16); o += D_SB
    cb = p[:, o:o + D_CONV]; o += D_CONV
    cc = p[:, o:o + D_CONV]; o += D_CONV
    cx = p[:, o:o + D_CONV]

    @pl.when(pl.program_id(0) % tiles_per_seq == 0)
    def _():
        z_scr[0:8, :] = jnp.zeros((8, D_CONV), F32)

    z = cc * cx
    z_scr[8:8 + tm, :] = z
    z1 = z_scr[7:7 + tm, :]
    z2 = z_scr[6:6 + tm, :]
    cw = cw_ref[...]
    y = cw[0:1, :] * z2 + cw[1:2, :] * z1 + cw[2:3, :] * z
    cp_ref[...] = (cb * y).astype(BF16)
    z_scr[0:8, :] = z_scr[tm:tm + 8, :]


def _proj(x2d, g, win, cw, *, seq):
    t, d = x2d.shape
    tm = BIG_TILE
    tok = lambda w: pl.BlockSpec((tm, w), lambda i: (i, 0))
    widths = [D_SB, D_SB, D_SB, D_CONV]
    lc = SSM_CHUNK
    return pl.pallas_call(
        functools.partial(_proj_kernel, tiles_per_seq=seq // tm),
        grid=(t // tm,),
        in_specs=[tok(d), _const_spec((1, d)), _const_spec(win.shape), _const_spec(cw.shape)],
        out_specs=[pl.BlockSpec((tm // lc, lc * D_SSM), lambda i: (i, 0))] + [tok(w) for w in widths],
        out_shape=[jax.ShapeDtypeStruct((t // lc, lc * D_SSM), BF16)]
        + [jax.ShapeDtypeStruct((t, w), BF16) for w in widths],
        scratch_shapes=[pltpu.VMEM((tm + 8, D_CONV), F32),
                        pltpu.VMEM((D_SSM // V7X_LANES, tm, V7X_LANES), F32)],
        compiler_params=pltpu.CompilerParams(dimension_semantics=("arbitrary",),
                                             vmem_limit_bytes=V7X_VMEM_LIMIT),
        name="proj",
    )(x2d, g, win, cw)


def _s5_kernel(u_ref, bexp_ref, cexp_ref, kt_ref, a_ref, d_ref, y_ref, e_scr, st_scr, y_scr):
    rows = u_ref.shape[0]
    half = st_scr.shape[1] // 2

    @pl.when(pl.program_id(1) == 0)
    def _():
        st_scr[...] = jnp.zeros(st_scr.shape, F32)

    u = u_ref[...]
    e_scr[...] = jnp.dot(u, bexp_ref[...], preferred_element_type=F32)
    a_re = a_ref[0:1, 0:half]
    a_im = a_ref[0:1, half:]

    def step(j, carry):
        s_re, s_im = carry
        e = e_scr[pl.ds(j, 1), :]
        e_scr[pl.ds(j, 1), :] = jnp.concatenate([s_re, s_im], axis=1)
        n_re = a_re * s_re - a_im * s_im + e[:, 0:half]
        n_im = a_re * s_im + a_im * s_re + e[:, half:]
        return n_re, n_im

    s_re, s_im = lax.fori_loop(0, rows, step, (st_scr[0:1, 0:half], st_scr[0:1, half:]))
    st_scr[0:1, 0:half] = s_re
    st_scr[0:1, half:] = s_im

    y = jnp.dot(e_scr[...].astype(BF16), cexp_ref[...], preferred_element_type=F32)
    y = y + jnp.dot(u, kt_ref[...], preferred_element_type=F32)
    y = y + d_ref[...] * u.astype(F32)
    y = jax.nn.gelu(y)
    for pos in range(SSM_CHUNK):
        for hf in range(D_SSM // V7X_LANES):
            lo = pos * D_SSM + hf * V7X_LANES
            y_scr[hf, pl.ds(pos, rows, stride=SSM_CHUNK), :] = y[:, lo:lo + V7X_LANES]
    for hf in range(D_SSM // V7X_LANES):
        y_ref[:, hf * V7X_LANES:(hf + 1) * V7X_LANES] = y_scr[hf].astype(BF16)


def _s5(u8, bexp, cexp, ktoep, a8, d8, *, batch):
    r, w = u8.shape
    rows = SSM_ROWS
    steps = r // batch // rows
    lc = SSM_CHUNK
    return pl.pallas_call(
        _s5_kernel,
        grid=(batch, steps),
        in_specs=[pl.BlockSpec((rows, w), lambda b, i: (b * steps + i, 0)),
                  _const_spec(bexp.shape), _const_spec(cexp.shape), _const_spec(ktoep.shape),
                  _const_spec(a8.shape), _const_spec(d8.shape)],
        out_specs=pl.BlockSpec((rows * lc, D_SSM), lambda b, i: (b * steps + i, 0)),
        out_shape=jax.ShapeDtypeStruct((r * lc, D_SSM), BF16),
        scratch_shapes=[pltpu.VMEM((rows, bexp.shape[1]), F32), pltpu.VMEM((8, bexp.shape[1]), F32),
                        pltpu.VMEM((D_SSM // V7X_LANES, rows * lc, V7X_LANES), F32)],
        compiler_params=pltpu.CompilerParams(dimension_semantics=("arbitrary", "arbitrary"),
                                             vmem_limit_bytes=V7X_VMEM_LIMIT),
        name="s5",
    )(u8, bexp, cexp, ktoep, a8, d8)


def _s5_matrices(a_re, a_im, log_dt, b_re, b_im, c_re, c_im, d_skip):
    hp = lax.Precision.HIGHEST
    lc = SSM_CHUNK
    dt = jnp.exp(log_dt)[:, None]
    mag = jnp.exp(a_re * dt)
    ab_re = mag * jnp.cos(a_im * dt)
    ab_im = mag * jnp.sin(a_im * dt)
    den = a_re * a_re + a_im * a_im
    xr = ab_re - 1.0
    coef_re = (xr * a_re + ab_im * a_im) / den
    coef_im = (ab_im * a_re - xr * a_im) / den
    bb_re = coef_re[..., None] * b_re - coef_im[..., None] * b_im
    bb_im = coef_re[..., None] * b_im + coef_im[..., None] * b_re
    p_re = [jnp.ones_like(ab_re)]
    p_im = [jnp.zeros_like(ab_im)]
    for _ in range(lc):
        r, i = p_re[-1], p_im[-1]
        p_re.append(r * ab_re - i * ab_im)
        p_im.append(r * ab_im + i * ab_re)
    p_re = jnp.stack(p_re)
    p_im = jnp.stack(p_im)
    gn = SSM_GROUPS * SSM_STATE
    lw = lc * D_SSM

    def expand(compact, row_w, col_w, cols):
        n_in = compact.shape[1]
        src = lax.broadcasted_iota(jnp.int32, (n_in, cols), 0)
        dst = lax.broadcasted_iota(jnp.int32, (n_in, cols), 1)
        per = cols // SSM_GROUPS // col_w
        sel = (src == (dst // (SSM_GROUPS * col_w)) * col_w + dst % col_w).astype(F32)
        assert per * col_w == n_in
        full = jnp.dot(compact, sel, precision=hp)
        rg = lax.broadcasted_iota(jnp.int32, full.shape, 0) // row_w % SSM_GROUPS
        cg = lax.broadcasted_iota(jnp.int32, full.shape, 1) // col_w % SSM_GROUPS
        return jnp.where(rg == cg, full, 0.0).astype(BF16)

    pr = p_re[lc - 1::-1][:lc]
    pi = p_im[lc - 1::-1][:lc]
    t_re = pr[:, :, :, None] * bb_re[None] - pi[:, :, :, None] * bb_im[None]
    t_im = pr[:, :, :, None] * bb_im[None] + pi[:, :, :, None] * bb_re[None]
    t1 = jnp.stack([t_re, t_im], axis=0)
    bexp = expand(t1.transpose(1, 2, 4, 0, 3).reshape(lw, 2 * SSM_STATE), SSM_GROUP, SSM_STATE, 2 * gn)

    qr = p_re[1:lc + 1]
    qi = p_im[1:lc + 1]
    ca_re = c_re[None] * qr[:, :, None, :] - c_im[None] * qi[:, :, None, :]
    ca_im = c_re[None] * qi[:, :, None, :] + c_im[None] * qr[:, :, None, :]
    t2 = jnp.stack([ca_re, -ca_im], axis=0)
    cexp = expand(t2.transpose(0, 2, 4, 1, 3).reshape(2 * gn, lc * SSM_GROUP), SSM_STATE, SSM_GROUP, lw)

    cp_re = c_re[None] * p_re[:lc, :, None, :] - c_im[None] * p_im[:lc, :, None, :]
    cp_im = c_re[None] * p_im[:lc, :, None, :] + c_im[None] * p_re[:lc, :, None, :]
    ktap = (jnp.einsum('tgcn,gnd->tgcd', cp_re, bb_re, precision=hp)
            - jnp.einsum('tgcn,gnd->tgcd', cp_im, bb_im, precision=hp))
    pos = jnp.arange(lc)
    tau = pos[None, :] - pos[:, None]
    taps = jnp.where((tau >= 0)[:, :, None, None, None], ktap[jnp.clip(tau, 0, lc - 1)], 0.0)
    ktoep = expand(taps.transpose(0, 2, 4, 1, 3).reshape(lw, lc * SSM_GROUP), SSM_GROUP, SSM_GROUP, lw)

    a8 = jnp.concatenate([p_re[lc].reshape(1, gn), p_im[lc].reshape(1, gn)], axis=1)
    d8 = jnp.tile(d_skip.reshape(1, D_SSM), (1, lc))
    return bexp.astype(BF16), cexp.astype(BF16), ktoep.astype(BF16), a8, d8


def _softplus(z):
    return jnp.maximum(z, 0.0) + jnp.log(1.0 + jnp.exp(-jnp.abs(z)))


def _attn_kernel(q_ref, k_ref, v_ref, o_ref):
    bq = q_ref.shape[0]
    lanes = V7X_LANES
    bk = bq
    i = pl.program_id(2)
    groups = range(ATT_GROUPS)
    lane = lax.broadcasted_iota(jnp.int32, (bq, lanes), 1)
    causal = (lax.broadcasted_iota(jnp.int32, (bq, bk), 1)
              < lax.broadcasted_iota(jnp.int32, (bq, bk), 0))
    causal = jnp.concatenate([causal] * HEADS_PER_STEP, axis=0)
    suffix = (lax.broadcasted_iota(jnp.int32, (bk, bk), 0)
              > lax.broadcasted_iota(jnp.int32, (bk, bk), 1)).astype(BF16)
    cols = [slice(g * lanes, (g + 1) * lanes) for g in groups]
    qs = [jnp.concatenate([jnp.where(lane // SB_HEAD_DIM == hh, q_ref[:, cols[g]], 0).astype(BF16)
                           for hh in range(HEADS_PER_STEP)], axis=0) for g in groups]

    def scores(g, kb, masked):
        rows = pl.ds(pl.multiple_of(kb * bk, bk), bk)
        z = lax.dot_general(qs[g], k_ref[rows, cols[g]], (((1,), (1,)), ((), ())),
                            preferred_element_type=F32)
        sp = _softplus(z)
        if masked:
            sp = jnp.where(causal, sp, 0.0)
        return z, sp

    def finish(g, kb, z, sp, c, run, acc, masked):
        rows = pl.ds(pl.multiple_of(kb * bk, bk), bk)
        w = jnp.exp(z - sp - c - jnp.concatenate([run] * (bk // lanes), axis=1))
        if masked:
            w = jnp.where(causal, w, 0.0)
        acc = acc + jnp.dot(w.astype(BF16), v_ref[rows, cols[g]], preferred_element_type=F32)
        return run + jnp.broadcast_to(c[:, 0:1] + sp[:, 0:1], run.shape), acc

    left = jnp.maximum(i - 1, 0)
    bias = jnp.where(i > 0, 0.0, NO_TILE_BIAS).astype(F32)
    first = [(g, kb, masked) for kb, masked in ((i, True), (left, False)) for g in groups]
    zs = [scores(*t) for t in first]
    c_all = jnp.dot(jnp.concatenate([sp.astype(BF16) for _, sp in zs], axis=0), suffix,
                    preferred_element_type=F32)
    rows2 = HEADS_PER_STEP * bq
    run = [jnp.zeros((rows2, lanes), F32) for _ in groups]
    acc = [jnp.zeros((rows2, lanes), F32) for _ in groups]
    for n, (g, kb, masked) in enumerate(first):
        r_in = run[g] if masked else run[g] + bias
        run[g], acc[g] = finish(g, kb, zs[n][0], zs[n][1], c_all[n * rows2:(n + 1) * rows2],
                                r_in, acc[g], masked)

    def more(st):
        j, runs, _ = st
        low = functools.reduce(jnp.minimum, [jnp.min(r) for r in runs])
        return jnp.logical_and(j <= i, low < F32_EXP_UNDERFLOW)

    def step(st):
        j, runs, accs = st
        out = []
        for g in groups:
            z, sp = scores(g, i - j, False)
            c = jnp.dot(sp.astype(BF16), suffix, preferred_element_type=F32)
            out.append(finish(g, i - j, z, sp, c, runs[g], accs[g], False))
        return j + 1, tuple(o[0] for o in out), tuple(o[1] for o in out)

    _, _, acc = lax.while_loop(more, step, (jnp.int32(2), tuple(run), tuple(acc)))
    for g in groups:
        o_ref[:, cols[g]] = jnp.where(lane < SB_HEAD_DIM, acc[g][0:bq], acc[g][bq:]).astype(BF16)


def _attn(q, k, v, *, batch, seq):
    bq = ATT_BLOCK
    nq = seq // bq
    assert HEADS_PER_STEP == 2
    wcols = ATT_GROUPS * V7X_LANES
    return pl.pallas_call(
        _attn_kernel,
        grid=(batch, D_SB // wcols, nq),
        in_specs=[pl.BlockSpec((bq, wcols), lambda b, p, i: (b * nq + i, p)),
                  pl.BlockSpec((seq, wcols), lambda b, p, i: (b, p)),
                  pl.BlockSpec((seq, wcols), lambda b, p, i: (b, p))],
        out_specs=pl.BlockSpec((bq, wcols), lambda b, p, i: (b * nq + i, p)),
        out_shape=jax.ShapeDtypeStruct((batch * seq, D_SB), BF16),
        compiler_params=pltpu.CompilerParams(
            dimension_semantics=("arbitrary", "arbitrary", "arbitrary"),
            vmem_limit_bytes=V7X_VMEM_LIMIT),
        name="attn",
    )(q, k, v)


def _merge_kernel(x_ref, ys_ref, oa_ref, cp_ref, gpre_ref, wg_ref, bg_ref, wval_ref, wglu_ref,
                  wao_ref, wco_ref, wmo_ref, gpost_ref, o_ref):
    d = x_ref.shape[1]
    x = x_ref[...]
    h = _rms(x, gpre_ref[...]).astype(BF16)
    gates = _sigmoid(jnp.dot(h, wg_ref[...], preferred_element_type=F32) + bg_ref[...])
    ys = ys_ref[...]
    y_ssm = (jnp.dot(ys, wval_ref[...], preferred_element_type=F32)
             * _sigmoid(jnp.dot(ys, wglu_ref[...], preferred_element_type=F32)))
    y_sb = jnp.dot(oa_ref[...], wao_ref[...], preferred_element_type=F32)
    y_conv = jnp.dot(cp_ref[...], wco_ref[...], preferred_element_type=F32)
    merged = gates[:, 0:d] * y_ssm + gates[:, d:2 * d] * y_sb + gates[:, 2 * d:] * y_conv
    m = jnp.dot(merged.astype(BF16), wmo_ref[...], preferred_element_type=F32)
    o_ref[...] = x + _rms(m, gpost_ref[...])


def _merge(x2d, ys, oa, cp, gpre, wg, bg, wval, wglu, wao, wco, wmo, gpost):
    t, d = x2d.shape
    tm = BIG_TILE
    tok = lambda w: pl.BlockSpec((tm, w), lambda i: (i, 0))
    consts = [gpre, wg, bg, wval, wglu, wao, wco, wmo, gpost]
    return pl.pallas_call(
        _merge_kernel,
        grid=(t // tm,),
        in_specs=[tok(d), tok(D_SSM), tok(D_SB), tok(D_CONV)] + [_const_spec(c.shape) for c in consts],
        out_specs=tok(d),
        out_shape=jax.ShapeDtypeStruct((t, d), F32),
        compiler_params=pltpu.CompilerParams(dimension_semantics=("arbitrary",),
                                             vmem_limit_bytes=V7X_VMEM_LIMIT),
        name="merge",
    )(x2d, ys, oa, cp, *consts)


def _ffn_kernel(x_ref, gpre_ref, wup_ref, wdn_ref, gpost_ref, o_ref):
    x = x_ref[...]
    h = _rms(x, gpre_ref[...]).astype(BF16)
    a = jnp.maximum(jnp.dot(h, wup_ref[...], preferred_element_type=F32), 0.0)
    f = jnp.dot((a * a).astype(BF16), wdn_ref[...], preferred_element_type=F32)
    o_ref[...] = x + _rms(f, gpost_ref[...])


def _ffn(x2d, gpre, wup, wdn, gpost):
    t, d = x2d.shape
    tm = TOK_TILE
    tok = pl.BlockSpec((tm, d), lambda i: (i, 0))
    consts = [gpre, wup, wdn, gpost]
    return pl.pallas_call(
        _ffn_kernel,
        grid=(t // tm,),
        in_specs=[tok] + [_const_spec(c.shape) for c in consts],
        out_specs=tok,
        out_shape=jax.ShapeDtypeStruct((t, d), F32),
        compiler_params=pltpu.CompilerParams(dimension_semantics=("arbitrary",),
                                             vmem_limit_bytes=V7X_VMEM_LIMIT),
        name="ffn",
    )(x2d, *consts)


def kernel(x, norm_mix_pre, norm_mix_post, w_in, w_gate, b_gate, ssm_a_re, ssm_a_im, ssm_log_dt,
           ssm_b_re, ssm_b_im, ssm_c_re, ssm_c_im, ssm_d, w_glu_val, w_glu_gate, w_attn_out, conv_w,
           w_conv_out, w_mix_out, norm_ffn_pre, norm_ffn_post, w_ffn_up, w_ffn_down):
    batch, seq, d = x.shape
    depth = w_in.shape[0]
    assert seq % (SSM_CHUNK * SSM_ROWS) == 0 and seq % BIG_TILE == 0 and seq % ATT_BLOCK == 0
    t = batch * seq
    row = lambda a: a.reshape(1, -1).astype(F32)
    x2d = x.reshape(t, d)
    for l in range(depth):
        u8, q, k, v, cp = _proj(x2d, row(norm_mix_pre[l]), w_in[l].astype(BF16), conv_w[l].astype(F32),
                                seq=seq)

        mats = _s5_matrices(ssm_a_re[l], ssm_a_im[l], ssm_log_dt[l], ssm_b_re[l], ssm_b_im[l],
                            ssm_c_re[l], ssm_c_im[l], ssm_d[l])
        ys = _s5(u8, *mats, batch=batch)

        oa = _attn(q, k, v, batch=batch, seq=seq)

        x2d = _merge(x2d, ys, oa, cp, row(norm_mix_pre[l]), w_gate[l].astype(BF16), row(b_gate[l]),
                     w_glu_val[l].astype(BF16), w_glu_gate[l].astype(BF16), w_attn_out[l].astype(BF16),
                     w_conv_out[l].astype(BF16), w_mix_out[l].astype(BF16), row(norm_mix_post[l]))
        x2d = _ffn(x2d, row(norm_ffn_pre[l]), w_ffn_up[l].astype(BF16), w_ffn_down[l].astype(BF16),
                   row(norm_ffn_post[l]))
    return x2d.reshape(batch, seq, d)
```

```python
import functools

import jax
import jax.numpy as jnp
from jax import lax
from jax.experimental import pallas as pl
from jax.experimental.pallas import tpu as pltpu

F32 = jnp.float32
BF16 = jnp.bfloat16

EPS = 1e-6
SSM_GROUP = 16
SSM_GROUPS = 16
D_SSM = SSM_GROUP * SSM_GROUPS
SSM_STATE = 64
SB_HEADS = 8
SB_HEAD_DIM = 64
D_SB = SB_HEADS * SB_HEAD_DIM
D_CONV = 256

V7X_LANES = 128
V7X_VMEM_LIMIT = 56 * 1024 * 1024

SSM_CHUNK = 8
SSM_ROWS = 256
FF_CHUNK = 1024
BIG_TILE = 1024
ATT_BLOCK = 256
HEADS_PER_STEP = V7X_LANES // SB_HEAD_DIM
ATT_GROUPS = 2
F32_EXP_UNDERFLOW = 104.0
NO_TILE_BIAS = 1e30


def _rms(x, g):
    return x * lax.rsqrt(jnp.mean(x * x, axis=-1, keepdims=True) + EPS) * g


def _sigmoid(x):
    return 1.0 / (1.0 + jnp.exp(-x))


def _const_spec(shape):
    return pl.BlockSpec(shape, lambda *_: (0,) * len(shape), pipeline_mode=pl.Buffered(1))


def _proj_kernel(x_ref, g_ref, win_ref, cw_ref, u8_ref, q_ref, k_ref, v_ref, cp_ref, z_scr, u_scr,
                 *, tiles_per_seq):
    tm = x_ref.shape[0]
    h = _rms(x_ref[...], g_ref[...]).astype(BF16)
    p = jnp.dot(h, win_ref[...], preferred_element_type=F32)
    o = 0
    for hf in range(D_SSM // V7X_LANES):
        u_scr[hf] = p[:, hf * V7X_LANES:(hf + 1) * V7X_LANES]
    for pos in range(SSM_CHUNK):
        for hf in range(D_SSM // V7X_LANES):
            lo = pos * D_SSM + hf * V7X_LANES
            u8_ref[:, lo:lo + V7X_LANES] = u_scr[
                hf, pl.ds(pos, tm // SSM_CHUNK, stride=SSM_CHUNK), :].astype(BF16)
    o += D_SSM
    q_ref[...] = (p[:, o:o + D_SB] * (SB_HEAD_DIM ** -0.5)).astype(BF16); o += D_SB
    k_ref[...] = p[:, o:o + D_SB].astype(BF16); o += D_SB
    v_ref[...] = p[:, o:o + D_SB].astype(BF16); o += D_SB
    cb = p[:, o:o + D_CONV]; o += D_CONV
    cc = p[:, o:o + D_CONV]; o += D_CONV
    cx = p[:, o:o + D_CONV]

    @pl.when(pl.program_id(0) % tiles_per_seq == 0)
    def _():
        z_scr[0:8, :] = jnp.zeros((8, D_CONV), F32)

    z = cc * cx
    z_scr[8:8 + tm, :] = z
    z1 = z_scr[7:7 + tm, :]
    z2 = z_scr[6:6 + tm, :]
    cw = cw_ref[...]
    y = cw[0:1, :] * z2 + cw[1:2, :] * z1 + cw[2:3, :] * z
    cp_ref[...] = (cb * y).astype(BF16)
    z_scr[0:8, :] = z_scr[tm:tm + 8, :]


def _proj(x2d, g, win, cw, *, seq):
    t, d = x2d.shape
    tm = BIG_TILE
    tok = lambda w: pl.BlockSpec((tm, w), lambda i: (i, 0))
    widths = [D_SB, D_SB, D_SB, D_CONV]
    lc = SSM_CHUNK
    return pl.pallas_call(
        functools.partial(_proj_kernel, tiles_per_seq=seq // tm),
        grid=(t // tm,),
        in_specs=[tok(d), _const_spec((1, d)), _const_spec(win.shape), _const_spec(cw.shape)],
        out_specs=[pl.BlockSpec((tm // lc, lc * D_SSM), lambda i: (i, 0))] + [tok(w) for w in widths],
        out_shape=[jax.ShapeDtypeStruct((t // lc, lc * D_SSM), BF16)]
        + [jax.ShapeDtypeStruct((t, w), BF16) for w in widths],
        scratch_shapes=[pltpu.VMEM((tm + 8, D_CONV), F32),
                        pltpu.VMEM((D_SSM // V7X_LANES, tm, V7X_LANES), F32)],
        compiler_params=pltpu.CompilerParams(dimension_semantics=("arbitrary",),
                                             vmem_limit_bytes=V7X_VMEM_LIMIT),
        name="proj",
    )(x2d, g, win, cw)


def _s5_kernel(u_ref, bexp_ref, cexp_ref, kt_ref, a_ref, d_ref, y_ref, e_scr, st_scr, y_scr):
    rows = u_ref.shape[0]
    half = st_scr.shape[1] // 2

    @pl.when(pl.program_id(1) == 0)
    def _():
        st_scr[...] = jnp.zeros(st_scr.shape, F32)

    u = u_ref[...]
    e_scr[...] = jnp.dot(u, bexp_ref[...], preferred_element_type=F32)
    a_re = a_ref[0:1, 0:half]
    a_im = a_ref[0:1, half:]

    def step(j, carry):
        s_re, s_im = carry
        e = e_scr[pl.ds(j, 1), :]
        e_scr[pl.ds(j, 1), :] = jnp.concatenate([s_re, s_im], axis=1)
        n_re = a_re * s_re - a_im * s_im + e[:, 0:half]
        n_im = a_re * s_im + a_im * s_re + e[:, half:]
        return n_re, n_im

    s_re, s_im = lax.fori_loop(0, rows, step, (st_scr[0:1, 0:half], st_scr[0:1, half:]))
    st_scr[0:1, 0:half] = s_re
    st_scr[0:1, half:] = s_im

    y = jnp.dot(e_scr[...].astype(BF16), cexp_ref[...], preferred_element_type=F32)
    y = y + jnp.dot(u, kt_ref[...], preferred_element_type=F32)
    y = y + d_ref[...] * u.astype(F32)
    y = jax.nn.gelu(y)
    for pos in range(SSM_CHUNK):
        for hf in range(D_SSM // V7X_LANES):
            lo = pos * D_SSM + hf * V7X_LANES
            y_scr[hf, pl.ds(pos, rows, stride=SSM_CHUNK), :] = y[:, lo:lo + V7X_LANES]
    for hf in range(D_SSM // V7X_LANES):
        y_ref[:, hf * V7X_LANES:(hf + 1) * V7X_LANES] = y_scr[hf].astype(BF16)


def _s5(u8, bexp, cexp, ktoep, a8, d8, *, batch):
    r, w = u8.shape
    rows = SSM_ROWS
    steps = r // batch // rows
    lc = SSM_CHUNK
    return pl.pallas_call(
        _s5_kernel,
        grid=(batch, steps),
        in_specs=[pl.BlockSpec((rows, w), lambda b, i: (b * steps + i, 0)),
                  _const_spec(bexp.shape), _const_spec(cexp.shape), _const_spec(ktoep.shape),
                  _const_spec(a8.shape), _const_spec(d8.shape)],
        out_specs=pl.BlockSpec((rows * lc, D_SSM), lambda b, i: (b * steps + i, 0)),
        out_shape=jax.ShapeDtypeStruct((r * lc, D_SSM), BF16),
        scratch_shapes=[pltpu.VMEM((rows, bexp.shape[1]), F32), pltpu.VMEM((8, bexp.shape[1]), F32),
                        pltpu.VMEM((D_SSM // V7X_LANES, rows * lc, V7X_LANES), F32)],
        compiler_params=pltpu.CompilerParams(dimension_semantics=("arbitrary", "arbitrary"),
                                             vmem_limit_bytes=V7X_VMEM_LIMIT),
        name="s5",
    )(u8, bexp, cexp, ktoep, a8, d8)


def _s5_matrices(a_re, a_im, log_dt, b_re, b_im, c_re, c_im, d_skip):
    hp = lax.Precision.HIGHEST
    lc = SSM_CHUNK
    dt = jnp.exp(log_dt)[:, None]
    mag = jnp.exp(a_re * dt)
    ab_re = mag * jnp.cos(a_im * dt)
    ab_im = mag * jnp.sin(a_im * dt)
    den = a_re * a_re + a_im * a_im
    xr = ab_re - 1.0
    coef_re = (xr * a_re + ab_im * a_im) / den
    coef_im = (ab_im * a_re - xr * a_im) / den
    bb_re = coef_re[..., None] * b_re - coef_im[..., None] * b_im
    bb_im = coef_re[..., None] * b_im + coef_im[..., None] * b_re
    p_re = [jnp.ones_like(ab_re)]
    p_im = [jnp.zeros_like(ab_im)]
    for _ in range(lc):
        r, i = p_re[-1], p_im[-1]
        p_re.append(r * ab_re - i * ab_im)
        p_im.append(r * ab_im + i * ab_re)
    p_re = jnp.stack(p_re)
    p_im = jnp.stack(p_im)
    gn = SSM_GROUPS * SSM_STATE
    lw = lc * D_SSM

    def expand(compact, row_w, col_w, cols):
        n_in = compact.shape[1]
        src = lax.broadcasted_iota(jnp.int32, (n_in, cols), 0)
        dst = lax.broadcasted_iota(jnp.int32, (n_in, cols), 1)
        per = cols // SSM_GROUPS // col_w
        sel = (src == (dst // (SSM_GROUPS * col_w)) * col_w + dst % col_w).astype(F32)
        assert per * col_w == n_in
        full = jnp.dot(compact, sel, precision=hp)
        rg = lax.broadcasted_iota(jnp.int32, full.shape, 0) // row_w % SSM_GROUPS
        cg = lax.broadcasted_iota(jnp.int32, full.shape, 1) // col_w % SSM_GROUPS
        return jnp.where(rg == cg, full, 0.0).astype(BF16)

    pr = p_re[lc - 1::-1][:lc]
    pi = p_im[lc - 1::-1][:lc]
    t_re = pr[:, :, :, None] * bb_re[None] - pi[:, :, :, None] * bb_im[None]
    t_im = pr[:, :, :, None] * bb_im[None] + pi[:, :, :, None] * bb_re[None]
    t1 = jnp.stack([t_re, t_im], axis=0)
    bexp = expand(t1.transpose(1, 2, 4, 0, 3).reshape(lw, 2 * SSM_STATE), SSM_GROUP, SSM_STATE, 2 * gn)

    qr = p_re[1:lc + 1]
    qi = p_im[1:lc + 1]
    ca_re = c_re[None] * qr[:, :, None, :] - c_im[None] * qi[:, :, None, :]
    ca_im = c_re[None] * qi[:, :, None, :] + c_im[None] * qr[:, :, None, :]
    t2 = jnp.stack([ca_re, -ca_im], axis=0)
    cexp = expand(t2.transpose(0, 2, 4, 1, 3).reshape(2 * gn, lc * SSM_GROUP), SSM_STATE, SSM_GROUP, lw)

    cp_re = c_re[None] * p_re[:lc, :, None, :] - c_im[None] * p_im[:lc, :, None, :]
    cp_im = c_re[None] * p_im[:lc, :, None, :] + c_im[None] * p_re[:lc, :, None, :]
    ktap = (jnp.einsum('tgcn,gnd->tgcd', cp_re, bb_re, precision=hp)
            - jnp.einsum('tgcn,gnd->tgcd', cp_im, bb_im, precision=hp))
    pos = jnp.arange(lc)
    tau = pos[None, :] - pos[:, None]
    taps = jnp.where((tau >= 0)[:, :, None, None, None], ktap[jnp.clip(tau, 0, lc - 1)], 0.0)
    ktoep = expand(taps.transpose(0, 2, 4, 1, 3).reshape(lw, lc * SSM_GROUP), SSM_GROUP, SSM_GROUP, lw)

    a8 = jnp.concatenate([p_re[lc].reshape(1, gn), p_im[lc].reshape(1, gn)], axis=1)
    d8 = jnp.tile(d_skip.reshape(1, D_SSM), (1, lc))
    return bexp.astype(BF16), cexp.astype(BF16), ktoep.astype(BF16), a8, d8


def _softplus(z):
    return jnp.maximum(z, 0.0) + jnp.log(1.0 + jnp.exp(-jnp.abs(z)))


def _attn_kernel(q_ref, k_ref, v_ref, o_ref):
    bq = q_ref.shape[0]
    lanes = V7X_LANES
    bk = bq
    i = pl.program_id(2)
    groups = range(ATT_GROUPS)
    lane = lax.broadcasted_iota(jnp.int32, (bq, lanes), 1)
    causal = (lax.broadcasted_iota(jnp.int32, (bq, bk), 1)
              < lax.broadcasted_iota(jnp.int32, (bq, bk), 0))
    causal = jnp.concatenate([causal] * HEADS_PER_STEP, axis=0)
    suffix = (lax.broadcasted_iota(jnp.int32, (bk, bk), 0)
              > lax.broadcasted_iota(jnp.int32, (bk, bk), 1)).astype(BF16)
    cols = [slice(g * lanes, (g + 1) * lanes) for g in groups]
    qs = [jnp.concatenate([jnp.where(lane // SB_HEAD_DIM == hh, q_ref[:, cols[g]], 0).astype(BF16)
                           for hh in range(HEADS_PER_STEP)], axis=0) for g in groups]

    def scores(g, kb, masked):
        rows = pl.ds(pl.multiple_of(kb * bk, bk), bk)
        z = lax.dot_general(qs[g], k_ref[rows, cols[g]], (((1,), (1,)), ((), ())),
                            preferred_element_type=F32)
        sp = _softplus(z)
        if masked:
            sp = jnp.where(causal, sp, 0.0)
        return z, sp

    def finish(g, kb, z, sp, c, run, acc, masked):
        rows = pl.ds(pl.multiple_of(kb * bk, bk), bk)
        w = jnp.exp(z - sp - c - jnp.concatenate([run] * (bk // lanes), axis=1))
        if masked:
            w = jnp.where(causal, w, 0.0)
        acc = acc + jnp.dot(w.astype(BF16), v_ref[rows, cols[g]], preferred_element_type=F32)
        return run + jnp.broadcast_to(c[:, 0:1] + sp[:, 0:1], run.shape), acc

    left = jnp.maximum(i - 1, 0)
    bias = jnp.where(i > 0, 0.0, NO_TILE_BIAS).astype(F32)
    first = [(g, kb, masked) for kb, masked in ((i, True), (left, False)) for g in groups]
    zs = [scores(*t) for t in first]
    c_all = jnp.dot(jnp.concatenate([sp.astype(BF16) for _, sp in zs], axis=0), suffix,
                    preferred_element_type=F32)
    rows2 = HEADS_PER_STEP * bq
    run = [jnp.zeros((rows2, lanes), F32) for _ in groups]
    acc = [jnp.zeros((rows2, lanes), F32) for _ in groups]
    for n, (g, kb, masked) in enumerate(first):
        r_in = run[g] if masked else run[g] + bias
        run[g], acc[g] = finish(g, kb, zs[n][0], zs[n][1], c_all[n * rows2:(n + 1) * rows2],
                                r_in, acc[g], masked)

    def more(st):
        j, runs, _ = st
        low = functools.reduce(jnp.minimum, [jnp.min(r) for r in runs])
        return jnp.logical_and(j <= i, low < F32_EXP_UNDERFLOW)

    def step(st):
        j, runs, accs = st
        out = []
        for g in groups:
            z, sp = scores(g, i - j, False)
            c = jnp.dot(sp.astype(BF16), suffix, preferred_element_type=F32)
            out.append(finish(g, i - j, z, sp, c, runs[g], accs[g], False))
        return j + 1, tuple(o[0] for o in out), tuple(o[1] for o in out)

    _, _, acc = lax.while_loop(more, step, (jnp.int32(2), tuple(run), tuple(acc)))
    for g in groups:
        o_ref[:, cols[g]] = jnp.where(lane < SB_HEAD_DIM, acc[g][0:bq], acc[g][bq:]).astype(BF16)


def _attn(q, k, v, *, batch, seq):
    bq = ATT_BLOCK
    nq = seq // bq
    assert HEADS_PER_STEP == 2
    wcols = ATT_GROUPS * V7X_LANES
    return pl.pallas_call(
        _attn_kernel,
        grid=(batch, D_SB // wcols, nq),
        in_specs=[pl.BlockSpec((bq, wcols), lambda b, p, i: (b * nq + i, p)),
                  pl.BlockSpec((seq, wcols), lambda b, p, i: (b, p)),
                  pl.BlockSpec((seq, wcols), lambda b, p, i: (b, p))],
        out_specs=pl.BlockSpec((bq, wcols), lambda b, p, i: (b * nq + i, p)),
        out_shape=jax.ShapeDtypeStruct((batch * seq, D_SB), BF16),
        compiler_params=pltpu.CompilerParams(
            dimension_semantics=("arbitrary", "arbitrary", "arbitrary"),
            vmem_limit_bytes=V7X_VMEM_LIMIT),
        name="attn",
    )(q, k, v)


def _merge_kernel(x_ref, ys_ref, oa_ref, cp_ref, gpre_ref, wg_ref, bg_ref, wval_ref, wglu_ref,
                  wao_ref, wco_ref, wmo_ref, gpost_ref, o_ref):
    d = x_ref.shape[1]
    x = x_ref[...]
    h = _rms(x, gpre_ref[...]).astype(BF16)
    gates = _sigmoid(jnp.dot(h, wg_ref[...], preferred_element_type=F32) + bg_ref[...])
    ys = ys_ref[...]
    y_ssm = (jnp.dot(ys, wval_ref[...], preferred_element_type=F32)
             * _sigmoid(jnp.dot(ys, wglu_ref[...], preferred_element_type=F32)))
    y_sb = jnp.dot(oa_ref[...], wao_ref[...], preferred_element_type=F32)
    y_conv = jnp.dot(cp_ref[...], wco_ref[...], preferred_element_type=F32)
    merged = gates[:, 0:d] * y_ssm + gates[:, d:2 * d] * y_sb + gates[:, 2 * d:] * y_conv
    m = jnp.dot(merged.astype(BF16), wmo_ref[...], preferred_element_type=F32)
    o_ref[...] = x + _rms(m, gpost_ref[...])


def _merge(x2d, ys, oa, cp, gpre, wg, bg, wval, wglu, wao, wco, wmo, gpost):
    t, d = x2d.shape
    tm = BIG_TILE
    tok = lambda w: pl.BlockSpec((tm, w), lambda i: (i, 0))
    consts = [gpre, wg, bg, wval, wglu, wao, wco, wmo, gpost]
    return pl.pallas_call(
        _merge_kernel,
        grid=(t // tm,),
        in_specs=[tok(d), tok(D_SSM), tok(D_SB), tok(D_CONV)] + [_const_spec(c.shape) for c in consts],
        out_specs=tok(d),
        out_shape=jax.ShapeDtypeStruct((t, d), F32),
        compiler_params=pltpu.CompilerParams(dimension_semantics=("arbitrary",),
                                             vmem_limit_bytes=V7X_VMEM_LIMIT),
        name="merge",
    )(x2d, ys, oa, cp, *consts)


def _ffn_kernel(x_ref, gpre_ref, wup_ref, wdn_ref, gpost_ref, o_ref):
    x = x_ref[...]
    h = _rms(x, gpre_ref[...]).astype(BF16)
    f = None
    for c in range(wup_ref.shape[1] // FF_CHUNK):
        cs = slice(c * FF_CHUNK, (c + 1) * FF_CHUNK)
        a = jnp.maximum(jnp.dot(h, wup_ref[:, cs], preferred_element_type=F32), 0.0)
        part = jnp.dot((a * a).astype(BF16), wdn_ref[cs, :], preferred_element_type=F32)
        f = part if f is None else f + part
    o_ref[...] = x + _rms(f, gpost_ref[...])


def _ffn(x2d, gpre, wup, wdn, gpost):
    t, d = x2d.shape
    tm = BIG_TILE
    tok = pl.BlockSpec((tm, d), lambda i: (i, 0))
    consts = [gpre, wup, wdn, gpost]
    return pl.pallas_call(
        _ffn_kernel,
        grid=(t // tm,),
        in_specs=[tok] + [_const_spec(c.shape) for c in consts],
        out_specs=tok,
        out_shape=jax.ShapeDtypeStruct((t, d), F32),
        compiler_params=pltpu.CompilerParams(dimension_semantics=("arbitrary",),
                                             vmem_limit_bytes=V7X_VMEM_LIMIT),
        name="ffn",
    )(x2d, *consts)


def kernel(x, norm_mix_pre, norm_mix_post, w_in, w_gate, b_gate, ssm_a_re, ssm_a_im, ssm_log_dt,
           ssm_b_re, ssm_b_im, ssm_c_re, ssm_c_im, ssm_d, w_glu_val, w_glu_gate, w_attn_out, conv_w,
           w_conv_out, w_mix_out, norm_ffn_pre, norm_ffn_post, w_ffn_up, w_ffn_down):
    batch, seq, d = x.shape
    depth = w_in.shape[0]
    assert seq % (SSM_CHUNK * SSM_ROWS) == 0 and seq % BIG_TILE == 0 and seq % ATT_BLOCK == 0
    t = batch * seq
    row = lambda a: a.reshape(1, -1).astype(F32)
    x2d = x.reshape(t, d)
    for l in range(depth):
        u8, q, k, v, cp = _proj(x2d, row(norm_mix_pre[l]), w_in[l].astype(BF16), conv_w[l].astype(F32),
                                seq=seq)

        mats = _s5_matrices(ssm_a_re[l], ssm_a_im[l], ssm_log_dt[l], ssm_b_re[l], ssm_b_im[l],
                            ssm_c_re[l], ssm_c_im[l], ssm_d[l])
        ys = _s5(u8, *mats, batch=batch)

        oa = _attn(q, k, v, batch=batch, seq=seq)

        x2d = _merge(x2d, ys, oa, cp, row(norm_mix_pre[l]), w_gate[l].astype(BF16), row(b_gate[l]),
                     w_glu_val[l].astype(BF16), w_glu_gate[l].astype(BF16), w_attn_out[l].astype(BF16),
                     w_conv_out[l].astype(BF16), w_mix_out[l].astype(BF16), row(norm_mix_post[l]))
        x2d = _ffn(x2d, row(norm_ffn_pre[l]), w_ffn_up[l].astype(BF16), w_ffn_down[l].astype(BF16),
                   row(norm_ffn_post[l]))
    return x2d.reshape(batch, seq, d)
```

```python
import functools

import jax
import jax.numpy as jnp
from jax import lax
from jax.experimental import pallas as pl
from jax.experimental.pallas import tpu as pltpu

F32 = jnp.float32
BF16 = jnp.bfloat16

EPS = 1e-6
SSM_GROUP = 16
SSM_GROUPS = 16
D_SSM = SSM_GROUP * SSM_GROUPS
SSM_STATE = 64
SB_HEADS = 8
SB_HEAD_DIM = 64
D_SB = SB_HEADS * SB_HEAD_DIM
D_CONV = 256

V7X_LANES = 128
V7X_VMEM_LIMIT = 56 * 1024 * 1024

SSM_CHUNK = 8
SSM_ROWS = 256
EXPAND_ROWS = 256
FF_CHUNK = 1024
BIG_TILE = 1024
ATT_BLOCK = 256
HEADS_PER_STEP = V7X_LANES // SB_HEAD_DIM
ATT_GROUPS = 2
F32_EXP_UNDERFLOW = 104.0
NO_TILE_BIAS = 1e30


def _rms(x, g):
    return x * lax.rsqrt(jnp.mean(x * x, axis=-1, keepdims=True) + EPS) * g


def _sigmoid(x):
    return 1.0 / (1.0 + jnp.exp(-x))


def _const_spec(shape):
    return pl.BlockSpec(shape, lambda *_: (0,) * len(shape), pipeline_mode=pl.Buffered(1))


def _proj_kernel(x_ref, g_ref, win_ref, cw_ref, u8_ref, q_ref, k_ref, v_ref, cp_ref, z_scr, u_scr,
                 *, tiles_per_seq):
    tm = x_ref.shape[0]
    h = _rms(x_ref[...], g_ref[...]).astype(BF16)
    p = jnp.dot(h, win_ref[...], preferred_element_type=F32)
    o = 0
    for hf in range(D_SSM // V7X_LANES):
        u_scr[hf] = p[:, hf * V7X_LANES:(hf + 1) * V7X_LANES]
    for pos in range(SSM_CHUNK):
        for hf in range(D_SSM // V7X_LANES):
            lo = pos * D_SSM + hf * V7X_LANES
            u8_ref[:, lo:lo + V7X_LANES] = u_scr[
                hf, pl.ds(pos, tm // SSM_CHUNK, stride=SSM_CHUNK), :].astype(BF16)
    o += D_SSM
    q_ref[...] = (p[:, o:o + D_SB] * (SB_HEAD_DIM ** -0.5)).astype(BF16); o += D_SB
    k_ref[...] = p[:, o:o + D_SB].astype(BF16); o += D_SB
    v_ref[...] = p[:, o:o + D_SB].astype(BF16); o += D_SB
    cb = p[:, o:o + D_CONV]; o += D_CONV
    cc = p[:, o:o + D_CONV]; o += D_CONV
    cx = p[:, o:o + D_CONV]

    @pl.when(pl.program_id(0) % tiles_per_seq == 0)
    def _():
        z_scr[0:8, :] = jnp.zeros((8, D_CONV), F32)

    z = cc * cx
    z_scr[8:8 + tm, :] = z
    z1 = z_scr[7:7 + tm, :]
    z2 = z_scr[6:6 + tm, :]
    cw = cw_ref[...]
    y = cw[0:1, :] * z2 + cw[1:2, :] * z1 + cw[2:3, :] * z
    cp_ref[...] = (cb * y).astype(BF16)
    z_scr[0:8, :] = z_scr[tm:tm + 8, :]


def _proj(x2d, g, win, cw, *, seq):
    t, d = x2d.shape
    tm = BIG_TILE
    tok = lambda w: pl.BlockSpec((tm, w), lambda i: (i, 0))
    widths = [D_SB, D_SB, D_SB, D_CONV]
    lc = SSM_CHUNK
    return pl.pallas_call(
        functools.partial(_proj_kernel, tiles_per_seq=seq // tm),
        grid=(t // tm,),
        in_specs=[tok(d), _const_spec((1, d)), _const_spec(win.shape), _const_spec(cw.shape)],
        out_specs=[pl.BlockSpec((tm // lc, lc * D_SSM), lambda i: (i, 0))] + [tok(w) for w in widths],
        out_shape=[jax.ShapeDtypeStruct((t // lc, lc * D_SSM), BF16)]
        + [jax.ShapeDtypeStruct((t, w), BF16) for w in widths],
        scratch_shapes=[pltpu.VMEM((tm + 8, D_CONV), F32),
                        pltpu.VMEM((D_SSM // V7X_LANES, tm, V7X_LANES), F32)],
        compiler_params=pltpu.CompilerParams(dimension_semantics=("arbitrary",),
                                             vmem_limit_bytes=V7X_VMEM_LIMIT),
        name="proj",
    )(x2d, g, win, cw)


def _expand_groups(dst_ref, compact_ref, row_w, col_w):
    n_in = compact_ref.shape[1]
    n_rows, cols = dst_ref.shape
    assert cols == n_in * SSM_GROUPS
    src = lax.broadcasted_iota(jnp.int32, (n_in, cols), 0)
    dst = lax.broadcasted_iota(jnp.int32, (n_in, cols), 1)
    sel = (src == dst // (SSM_GROUPS * col_w) * col_w + dst % col_w).astype(BF16)
    for r0 in range(0, n_rows, EXPAND_ROWS):
        full = jnp.dot(compact_ref[r0:r0 + EXPAND_ROWS, :], sel, preferred_element_type=F32)
        rg = (lax.broadcasted_iota(jnp.int32, full.shape, 0) + r0) // row_w % SSM_GROUPS
        cg = lax.broadcasted_iota(jnp.int32, full.shape, 1) // col_w % SSM_GROUPS
        dst_ref[r0:r0 + EXPAND_ROWS, :] = jnp.where(rg == cg, full, 0.0).astype(BF16)


def _s5_kernel(u_ref, bc_ref, cc_ref, kc_ref, a_ref, d_ref, y_ref,
               e_scr, st_scr, y_scr, bexp_ref, cexp_ref, kt_ref):
    rows = u_ref.shape[0]
    half = st_scr.shape[1] // 2

    @pl.when(jnp.logical_and(pl.program_id(0) == 0, pl.program_id(1) == 0))
    def _():
        _expand_groups(bexp_ref, bc_ref, SSM_GROUP, SSM_STATE)
        _expand_groups(cexp_ref, cc_ref, SSM_STATE, SSM_GROUP)
        _expand_groups(kt_ref, kc_ref, SSM_GROUP, SSM_GROUP)

    @pl.when(pl.program_id(1) == 0)
    def _():
        st_scr[...] = jnp.zeros(st_scr.shape, F32)

    u = u_ref[...]
    e_scr[...] = jnp.dot(u, bexp_ref[...], preferred_element_type=F32)
    a_re = a_ref[0:1, 0:half]
    a_im = a_ref[0:1, half:]

    def step(j, carry):
        s_re, s_im = carry
        e = e_scr[pl.ds(j, 1), :]
        e_scr[pl.ds(j, 1), :] = jnp.concatenate([s_re, s_im], axis=1)
        n_re = a_re * s_re - a_im * s_im + e[:, 0:half]
        n_im = a_re * s_im + a_im * s_re + e[:, half:]
        return n_re, n_im

    s_re, s_im = lax.fori_loop(0, rows, step, (st_scr[0:1, 0:half], st_scr[0:1, half:]))
    st_scr[0:1, 0:half] = s_re
    st_scr[0:1, half:] = s_im

    y = jnp.dot(e_scr[...].astype(BF16), cexp_ref[...], preferred_element_type=F32)
    y = y + jnp.dot(u, kt_ref[...], preferred_element_type=F32)
    y = y + d_ref[...] * u.astype(F32)
    y = jax.nn.gelu(y)
    for pos in range(SSM_CHUNK):
        for hf in range(D_SSM // V7X_LANES):
            lo = pos * D_SSM + hf * V7X_LANES
            y_scr[hf, pl.ds(pos, rows, stride=SSM_CHUNK), :] = y[:, lo:lo + V7X_LANES]
    for hf in range(D_SSM // V7X_LANES):
        y_ref[:, hf * V7X_LANES:(hf + 1) * V7X_LANES] = y_scr[hf].astype(BF16)


def _s5(u8, bexp_c, cexp_c, ktoep_c, a8, d8, *, batch):
    r, w = u8.shape
    rows = SSM_ROWS
    steps = r // batch // rows
    lc = SSM_CHUNK
    ns = a8.shape[1]
    return pl.pallas_call(
        _s5_kernel,
        grid=(batch, steps),
        in_specs=[pl.BlockSpec((rows, w), lambda b, i: (b * steps + i, 0)),
                  _const_spec(bexp_c.shape), _const_spec(cexp_c.shape), _const_spec(ktoep_c.shape),
                  _const_spec(a8.shape), _const_spec(d8.shape)],
        out_specs=pl.BlockSpec((rows * lc, D_SSM), lambda b, i: (b * steps + i, 0)),
        out_shape=jax.ShapeDtypeStruct((r * lc, D_SSM), BF16),
        scratch_shapes=[pltpu.VMEM((rows, ns), F32), pltpu.VMEM((8, ns), F32),
                        pltpu.VMEM((D_SSM // V7X_LANES, rows * lc, V7X_LANES), F32),
                        pltpu.VMEM((w, ns), BF16), pltpu.VMEM((ns, w), BF16), pltpu.VMEM((w, w), BF16)],
        compiler_params=pltpu.CompilerParams(dimension_semantics=("arbitrary", "arbitrary"),
                                             vmem_limit_bytes=V7X_VMEM_LIMIT),
        name="s5",
    )(u8, bexp_c, cexp_c, ktoep_c, a8, d8)


def _s5_matrices(a_re, a_im, log_dt, b_re, b_im, c_re, c_im, d_skip):
    hp = lax.Precision.HIGHEST
    lc = SSM_CHUNK
    dt = jnp.exp(log_dt)[:, None]
    mag = jnp.exp(a_re * dt)
    ab_re = mag * jnp.cos(a_im * dt)
    ab_im = mag * jnp.sin(a_im * dt)
    den = a_re * a_re + a_im * a_im
    xr = ab_re - 1.0
    coef_re = (xr * a_re + ab_im * a_im) / den
    coef_im = (ab_im * a_re - xr * a_im) / den
    bb_re = coef_re[..., None] * b_re - coef_im[..., None] * b_im
    bb_im = coef_re[..., None] * b_im + coef_im[..., None] * b_re
    p_re = [jnp.ones_like(ab_re)]
    p_im = [jnp.zeros_like(ab_im)]
    for _ in range(lc):
        r, i = p_re[-1], p_im[-1]
        p_re.append(r * ab_re - i * ab_im)
        p_im.append(r * ab_im + i * ab_re)
    p_re = jnp.stack(p_re)
    p_im = jnp.stack(p_im)
    gn = SSM_GROUPS * SSM_STATE
    lw = lc * D_SSM

    pr = p_re[lc - 1::-1][:lc]
    pi = p_im[lc - 1::-1][:lc]
    t_re = pr[:, :, :, None] * bb_re[None] - pi[:, :, :, None] * bb_im[None]
    t_im = pr[:, :, :, None] * bb_im[None] + pi[:, :, :, None] * bb_re[None]
    t1 = jnp.stack([t_re, t_im], axis=0)
    bexp = t1.transpose(1, 2, 4, 0, 3).reshape(lw, 2 * SSM_STATE)

    qr = p_re[1:lc + 1]
    qi = p_im[1:lc + 1]
    ca_re = c_re[None] * qr[:, :, None, :] - c_im[None] * qi[:, :, None, :]
    ca_im = c_re[None] * qi[:, :, None, :] + c_im[None] * qr[:, :, None, :]
    t2 = jnp.stack([ca_re, -ca_im], axis=0)
    cexp = t2.transpose(0, 2, 4, 1, 3).reshape(2 * gn, lc * SSM_GROUP)

    cp_re = c_re[None] * p_re[:lc, :, None, :] - c_im[None] * p_im[:lc, :, None, :]
    cp_im = c_re[None] * p_im[:lc, :, None, :] + c_im[None] * p_re[:lc, :, None, :]
    ktap = (jnp.einsum('tgcn,gnd->tgcd', cp_re, bb_re, precision=hp)
            - jnp.einsum('tgcn,gnd->tgcd', cp_im, bb_im, precision=hp))
    pos = jnp.arange(lc)
    tau = pos[None, :] - pos[:, None]
    taps = jnp.where((tau >= 0)[:, :, None, None, None], ktap[jnp.clip(tau, 0, lc - 1)], 0.0)
    ktoep = taps.transpose(0, 2, 4, 1, 3).reshape(lw, lc * SSM_GROUP)

    a8 = jnp.concatenate([p_re[lc].reshape(1, gn), p_im[lc].reshape(1, gn)], axis=1)
    d8 = jnp.tile(d_skip.reshape(1, D_SSM), (1, lc))
    return bexp.astype(BF16), cexp.astype(BF16), ktoep.astype(BF16), a8, d8


def _softplus(z):
    return jnp.maximum(z, 0.0) + jnp.log(1.0 + jnp.exp(-jnp.abs(z)))


def _attn_kernel(q_ref, k_ref, v_ref, o_ref):
    bq = q_ref.shape[0]
    lanes = V7X_LANES
    bk = bq
    i = pl.program_id(2)
    groups = range(ATT_GROUPS)
    lane = lax.broadcasted_iota(jnp.int32, (bq, lanes), 1)
    causal = (lax.broadcasted_iota(jnp.int32, (bq, bk), 1)
              < lax.broadcasted_iota(jnp.int32, (bq, bk), 0))
    causal = jnp.concatenate([causal] * HEADS_PER_STEP, axis=0)
    suffix = (lax.broadcasted_iota(jnp.int32, (bk, bk), 0)
              > lax.broadcasted_iota(jnp.int32, (bk, bk), 1)).astype(BF16)
    cols = [slice(g * lanes, (g + 1) * lanes) for g in groups]
    qs = [jnp.concatenate([jnp.where(lane // SB_HEAD_DIM == hh, q_ref[:, cols[g]], 0).astype(BF16)
                           for hh in range(HEADS_PER_STEP)], axis=0) for g in groups]

    def scores(g, kb, masked):
        rows = pl.ds(pl.multiple_of(kb * bk, bk), bk)
        z = lax.dot_general(qs[g], k_ref[rows, cols[g]], (((1,), (1,)), ((), ())),
                            preferred_element_type=F32)
        sp = _softplus(z)
        if masked:
            sp = jnp.where(causal, sp, 0.0)
        return z, sp

    def finish(g, kb, z, sp, c, run, acc, masked):
        rows = pl.ds(pl.multiple_of(kb * bk, bk), bk)
        w = jnp.exp(z - sp - c - jnp.concatenate([run] * (bk // lanes), axis=1))
        if masked:
            w = jnp.where(causal, w, 0.0)
        acc = acc + jnp.dot(w.astype(BF16), v_ref[rows, cols[g]], preferred_element_type=F32)
        return run + jnp.broadcast_to(c[:, 0:1] + sp[:, 0:1], run.shape), acc

    left = jnp.maximum(i - 1, 0)
    bias = jnp.where(i > 0, 0.0, NO_TILE_BIAS).astype(F32)
    first = [(g, kb, masked) for kb, masked in ((i, True), (left, False)) for g in groups]
    zs = [scores(*t) for t in first]
    c_all = jnp.dot(jnp.concatenate([sp.astype(BF16) for _, sp in zs], axis=0), suffix,
                    preferred_element_type=F32)
    rows2 = HEADS_PER_STEP * bq
    run = [jnp.zeros((rows2, lanes), F32) for _ in groups]
    acc = [jnp.zeros((rows2, lanes), F32) for _ in groups]
    for n, (g, kb, masked) in enumerate(first):
        r_in = run[g] if masked else run[g] + bias
        run[g], acc[g] = finish(g, kb, zs[n][0], zs[n][1], c_all[n * rows2:(n + 1) * rows2],
                                r_in, acc[g], masked)

    def more(st):
        j, runs, _ = st
        low = functools.reduce(jnp.minimum, [jnp.min(r) for r in runs])
        return jnp.logical_and(j <= i, low < F32_EXP_UNDERFLOW)

    def step(st):
        j, runs, accs = st
        out = []
        for g in groups:
            z, sp = scores(g, i - j, False)
            c = jnp.dot(sp.astype(BF16), suffix, preferred_element_type=F32)
            out.append(finish(g, i - j, z, sp, c, runs[g], accs[g], False))
        return j + 1, tuple(o[0] for o in out), tuple(o[1] for o in out)

    _, _, acc = lax.while_loop(more, step, (jnp.int32(2), tuple(run), tuple(acc)))
    for g in groups:
        o_ref[:, cols[g]] = jnp.where(lane < SB_HEAD_DIM, acc[g][0:bq], acc[g][bq:]).astype(BF16)


def _attn(q, k, v, *, batch, seq):
    bq = ATT_BLOCK
    nq = seq // bq
    assert HEADS_PER_STEP == 2
    wcols = ATT_GROUPS * V7X_LANES
    return pl.pallas_call(
        _attn_kernel,
        grid=(batch, D_SB // wcols, nq),
        in_specs=[pl.BlockSpec((bq, wcols), lambda b, p, i: (b * nq + i, p)),
                  pl.BlockSpec((seq, wcols), lambda b, p, i: (b, p)),
                  pl.BlockSpec((seq, wcols), lambda b, p, i: (b, p))],
        out_specs=pl.BlockSpec((bq, wcols), lambda b, p, i: (b * nq + i, p)),
        out_shape=jax.ShapeDtypeStruct((batch * seq, D_SB), BF16),
        compiler_params=pltpu.CompilerParams(
            dimension_semantics=("arbitrary", "arbitrary", "arbitrary"),
            vmem_limit_bytes=V7X_VMEM_LIMIT),
        name="attn",
    )(q, k, v)


def _merge_kernel(x_ref, ys_ref, oa_ref, cp_ref, gpre_ref, wg_ref, bg_ref, wval_ref, wglu_ref,
                  wao_ref, wco_ref, wmo_ref, gpost_ref, o_ref):
    d = x_ref.shape[1]
    x = x_ref[...]
    h = _rms(x, gpre_ref[...]).astype(BF16)
    gates = _sigmoid(jnp.dot(h, wg_ref[...], preferred_element_type=F32) + bg_ref[...])
    ys = ys_ref[...]
    y_ssm = (jnp.dot(ys, wval_ref[...], preferred_element_type=F32)
             * _sigmoid(jnp.dot(ys, wglu_ref[...], preferred_element_type=F32)))
    y_sb = jnp.dot(oa_ref[...], wao_ref[...], preferred_element_type=F32)
    y_conv = jnp.dot(cp_ref[...], wco_ref[...], preferred_element_type=F32)
    merged = gates[:, 0:d] * y_ssm + gates[:, d:2 * d] * y_sb + gates[:, 2 * d:] * y_conv
    m = jnp.dot(merged.astype(BF16), wmo_ref[...], preferred_element_type=F32)
    o_ref[...] = x + _rms(m, gpost_ref[...])


def _merge(x2d, ys, oa, cp, gpre, wg, bg, wval, wglu, wao, wco, wmo, gpost):
    t, d = x2d.shape
    tm = BIG_TILE
    tok = lambda w: pl.BlockSpec((tm, w), lambda i: (i, 0))
    consts = [gpre, wg, bg, wval, wglu, wao, wco, wmo, gpost]
    return pl.pallas_call(
        _merge_kernel,
        grid=(t // tm,),
        in_specs=[tok(d), tok(D_SSM), tok(D_SB), tok(D_CONV)] + [_const_spec(c.shape) for c in consts],
        out_specs=tok(d),
        out_shape=jax.ShapeDtypeStruct((t, d), F32),
        compiler_params=pltpu.CompilerParams(dimension_semantics=("arbitrary",),
                                             vmem_limit_bytes=V7X_VMEM_LIMIT),
        name="merge",
    )(x2d, ys, oa, cp, *consts)


def _ffn_kernel(x_ref, gpre_ref, wup_ref, wdn_ref, gpost_ref, o_ref):
    x = x_ref[...]
    h = _rms(x, gpre_ref[...]).astype(BF16)
    f = None
    for c in range(wup_ref.shape[1] // FF_CHUNK):
        cs = slice(c * FF_CHUNK, (c + 1) * FF_CHUNK)
        a = jnp.maximum(jnp.dot(h, wup_ref[:, cs], preferred_element_type=F32), 0.0)
        part = jnp.dot((a * a).astype(BF16), wdn_ref[cs, :], preferred_element_type=F32)
        f = part if f is None else f + part
    o_ref[...] = x + _rms(f, gpost_ref[...])


def _ffn(x2d, gpre, wup, wdn, gpost):
    t, d = x2d.shape
    tm = BIG_TILE
    tok = pl.BlockSpec((tm, d), lambda i: (i, 0))
    consts = [gpre, wup, wdn, gpost]
    return pl.pallas_call(
        _ffn_kernel,
        grid=(t // tm,),
        in_specs=[tok] + [_const_spec(c.shape) for c in consts],
        out_specs=tok,
        out_shape=jax.ShapeDtypeStruct((t, d), F32),
        compiler_params=pltpu.CompilerParams(dimension_semantics=("arbitrary",),
                                             vmem_limit_bytes=V7X_VMEM_LIMIT),
        name="ffn",
    )(x2d, *consts)


def kernel(x, norm_mix_pre, norm_mix_post, w_in, w_gate, b_gate, ssm_a_re, ssm_a_im, ssm_log_dt,
           ssm_b_re, ssm_b_im, ssm_c_re, ssm_c_im, ssm_d, w_glu_val, w_glu_gate, w_attn_out, conv_w,
           w_conv_out, w_mix_out, norm_ffn_pre, norm_ffn_post, w_ffn_up, w_ffn_down):
    batch, seq, d = x.shape
    depth = w_in.shape[0]
    assert seq % (SSM_CHUNK * SSM_ROWS) == 0 and seq % BIG_TILE == 0 and seq % ATT_BLOCK == 0
    t = batch * seq
    row = lambda a: a.reshape(1, -1).astype(F32)
    x2d = x.reshape(t, d)
    for l in range(depth):
        u8, q, k, v, cp = _proj(x2d, row(norm_mix_pre[l]), w_in[l].astype(BF16), conv_w[l].astype(F32),
                                seq=seq)

        mats = _s5_matrices(ssm_a_re[l], ssm_a_im[l], ssm_log_dt[l], ssm_b_re[l], ssm_b_im[l],
                            ssm_c_re[l], ssm_c_im[l], ssm_d[l])
        ys = _s5(u8, *mats, batch=batch)

        oa = _attn(q, k, v, batch=batch, seq=seq)

        x2d = _merge(x2d, ys, oa, cp, row(norm_mix_pre[l]), w_gate[l].astype(BF16), row(b_gate[l]),
                     w_glu_val[l].astype(BF16), w_glu_gate[l].astype(BF16), w_attn_out[l].astype(BF16),
                     w_conv_out[l].astype(BF16), w_mix_out[l].astype(BF16), row(norm_mix_post[l]))
        x2d = _ffn(x2d, row(norm_ffn_pre[l]), w_ffn_up[l].astype(BF16), w_ffn_down[l].astype(BF16),
                   row(norm_ffn_post[l]))
    return x2d.reshape(batch, seq, d)
```
